```python
import math
import jax, jax.numpy as jnp
from jax import lax
import numpy as np

D_MODEL = 2048
BATCH = 16
SEQ = 256
DEPTH = 2
DEC_BATCH = 8
DEC_SEQ = 1024
PAST_LEN = 512

GRID_W = 64
N_EVEN = (DEPTH + 1) // 2
N_ODD = DEPTH // 2
EPS = 1e-6
N_MOD = 9
D_FF = 5632
POOL_WINDOWS = (2, 4, 8, 16)
N_POOL_GROUPS = 4
POOL_WIDTH = D_MODEL // 2
POOL_GROUP_DIM = POOL_WIDTH // N_POOL_GROUPS
HEAD_DIM = 128
N_Q_HEADS = (D_MODEL // 2) // HEAD_DIM
N_KV_HEADS = 2
Q_PER_KV = N_Q_HEADS // N_KV_HEADS
ATTN_WIDTH = N_Q_HEADS * HEAD_DIM
KV_WIDTH = N_KV_HEADS * HEAD_DIM
MIX_IN = POOL_WIDTH + ATTN_WIDTH + 2 * KV_WIDTH
MIX_OUT = POOL_WIDTH + ATTN_WIDTH
Q_BLOCK = 128
ROPE_THETA = 10000.0
D_INNER = 2 * D_MODEL
SSM_HEADDIM = 64
SSM_HEADS = D_INNER // SSM_HEADDIM
SSM_GROUPS = 8
HEADS_PER_GROUP = SSM_HEADS // SSM_GROUPS
D_STATE = 128
D_CONV = 3
SSM_CHUNK = 128
CONV_DIM = D_INNER + 2 * SSM_GROUPS * D_STATE
SSM_IN = D_INNER + CONV_DIM + 2 * SSM_HEADS

kernel_name = "hybrid_pool_gqa_ssd_diffusion_step"

F32 = jnp.float32


def rmsnorm(x, g):
    x32 = x.astype(F32)
    r = x32 * lax.rsqrt(jnp.mean(x32 * x32, axis=-1, keepdims=True) + EPS)
    return (r * g.astype(F32)).astype(x.dtype)


def modulate_in(h, g_pre, shift, scale):
    return rmsnorm(h, g_pre) * (1 + scale[:, None]) + shift[:, None]


def residual_add(h, y, g_post, gate, w):
    return h + w * gate[:, None] * rmsnorm(y, g_post)


def swiglu(u, w_in, w_out):
    a, b = jnp.split(u @ w_in, 2, axis=-1)
    return (jax.nn.silu(a) * b) @ w_out


def pool_mixer(xp, pool_w_l, pool_scale_l):
    b, L, _ = xp.shape
    x32 = xp.astype(F32)
    cs = jnp.concatenate([jnp.zeros((b, 1, POOL_WIDTH), F32), jnp.cumsum(x32, axis=1)], axis=1)
    cs = cs.reshape(b, L + 1, N_POOL_GROUPS, POOL_GROUP_DIM)
    xg = x32.reshape(b, L, N_POOL_GROUPS, POOL_GROUP_DIM)
    t = jnp.arange(L)
    outs = []
    for gi, w in enumerate(POOL_WINDOWS):
        lo = jnp.clip(t - w // 2, 0, L)
        hi = jnp.clip(t - w // 2 + w, 0, L)
        s = cs[:, hi, gi] - cs[:, lo, gi]
        mean = s / (hi - lo).astype(F32)[None, :, None]
        outs.append(mean - xg[:, :, gi])
    d = jnp.stack(outs, axis=2).astype(xp.dtype)
    y = jnp.einsum('blgc,gcd->blgd', d, pool_w_l).reshape(b, L, POOL_WIDTH)
    return y * pool_scale_l


def rope_2d(x):
    L = x.shape[1]
    rows = L // GRID_W
    row = jnp.repeat(jnp.arange(rows), GRID_W).astype(F32)
    col = jnp.tile(jnp.arange(GRID_W), rows).astype(F32)
    half = HEAD_DIM // 2
    quarter = half // 2
    inv_freq = ROPE_THETA ** (-jnp.arange(quarter, dtype=F32) / quarter)

    def rot(xh, pos):
        ang = pos[:, None] * inv_freq[None]
        cos = jnp.cos(ang)[None, :, None]
        sin = jnp.sin(ang)[None, :, None]
        x1, x2 = xh[..., :quarter], xh[..., quarter:]
        return jnp.concatenate([x1 * cos - x2 * sin, x2 * cos + x1 * sin], axis=-1)

    x32 = x.astype(F32)
    return jnp.concatenate([rot(x32[..., :half], row), rot(x32[..., half:], col)], axis=-1).astype(x.dtype)


def block_attention(q, k, v):
    b, Lq, H, Dh = q.shape
    nb = Lq // Q_BLOCK
    qb = q.reshape(b, nb, Q_BLOCK, N_KV_HEADS, Q_PER_KV, Dh).transpose(1, 0, 2, 3, 4, 5)
    scale = HEAD_DIM ** -0.5

    def one(qblk):
        s = jnp.einsum('bqkgd,bskd->bkgqs', qblk, k, preferred_element_type=F32) * scale
        p = jax.nn.softmax(s, axis=-1)
        o = jnp.einsum('bkgqs,bskd->bqkgd', p.astype(v.dtype), v, preferred_element_type=F32)
        return o.astype(q.dtype)

    o = lax.map(one, qb)
    return o.transpose(1, 0, 2, 3, 4, 5).reshape(b, Lq, H * Dh)


def even_mixer(u, w_in, pool_w_l, pool_scale_l, qk_g, w_out, ctx_kv):
    b, L, _ = u.shape
    p = u @ w_in
    xp, q, k, v = jnp.split(p, [POOL_WIDTH, POOL_WIDTH + ATTN_WIDTH, POOL_WIDTH + ATTN_WIDTH + KV_WIDTH], axis=-1)
    q = rmsnorm(q.reshape(b, L, N_Q_HEADS, HEAD_DIM), qk_g[0])
    k = rmsnorm(k.reshape(b, L, N_KV_HEADS, HEAD_DIM), qk_g[1])
    v = v.reshape(b, L, N_KV_HEADS, HEAD_DIM)
    if ctx_kv is None:
        attn = block_attention(q, k, v)
    else:
        ck, cv = ctx_kv
        k_all = jnp.concatenate([ck.astype(k.dtype), rope_2d(k)], axis=1)
        v_all = jnp.concatenate([cv.astype(v.dtype), v], axis=1)
        attn = block_attention(rope_2d(q), k_all, v_all)
    y = jnp.concatenate([pool_mixer(xp, pool_w_l, pool_scale_l), attn], axis=-1) @ w_out
    return y, k, v


def centred_conv(x, w, bias):
    L = x.shape[1]
    pad = D_CONV // 2
    xpad = jnp.pad(x, ((0, 0), (pad, D_CONV - 1 - pad), (0, 0)))
    out = xpad[:, 0:L] * w[:, 0]
    for j in range(1, D_CONV):
        out = out + xpad[:, j:j + L] * w[:, j]
    return out + bias


def ssd_scan(x, dt, A, Bm, Cm, h0):
    b, L, H, P = x.shape
    nc = L // SSM_CHUNK
    Q = SSM_CHUNK

    def to_chunks(a):
        return jnp.moveaxis(a.reshape(b, nc, Q, *a.shape[2:]), 1, 0)

    xs = to_chunks(x.astype(F32))
    dts = to_chunks(dt)
    Bs = to_chunks(Bm.astype(F32))
    Cs = to_chunks(Cm.astype(F32))
    mask = jnp.tril(jnp.ones((Q, Q), bool))

    def step(h, inp):
        xc, dtc, Bc, Cc = inp
        a = jnp.cumsum(dtc * A, axis=1)
        diff = a[:, :, None, :] - a[:, None, :, :]
        decay = jnp.exp(jnp.where(mask[None, :, :, None], diff, -jnp.inf))
        decay = decay.transpose(0, 3, 1, 2).reshape(b, SSM_GROUPS, HEADS_PER_GROUP, Q, Q)
        cb = jnp.einsum('bign,bjgn->bgij', Cc, Bc)
        wmat = cb[:, :, None] * decay
        xdt = (xc * dtc[..., None]).reshape(b, Q, SSM_GROUPS, HEADS_PER_GROUP, P)
        y_intra = jnp.einsum('bghij,bjghp->bighp', wmat, xdt)
        hg = h.reshape(b, SSM_GROUPS, HEADS_PER_GROUP, P, D_STATE)
        y_state = jnp.einsum('bign,bghpn->bighp', Cc, hg) * jnp.exp(a).reshape(b, Q, SSM_GROUPS, HEADS_PER_GROUP)[..., None]
        dend = jnp.exp(a[:, -1:, :] - a).reshape(b, Q, SSM_GROUPS, HEADS_PER_GROUP)[..., None]
        h_new = h * jnp.exp(a[:, -1])[:, :, None, None] + jnp.einsum('bjgn,bjghp->bghpn', Bc, xdt * dend).reshape(b, H, P, D_STATE)
        return h_new, (y_intra + y_state).reshape(b, Q, H, P)

    hT, ys = lax.scan(step, h0.astype(F32), (xs, dts, Bs, Cs))
    return jnp.moveaxis(ys, 0, 1).reshape(b, L, H, P), hT


def odd_mixer(u, w_in, conv_w, conv_b, dt_bias, A_log, D_skip, norm_g, w_out, h0):
    b, L, _ = u.shape
    z, xbc, dt = jnp.split(u @ w_in, [D_INNER, D_INNER + CONV_DIM], axis=-1)
    xbc = jax.nn.silu(centred_conv(xbc, conv_w, conv_b))
    xs, Bm, Cm = jnp.split(xbc, [D_INNER, D_INNER + SSM_GROUPS * D_STATE], axis=-1)
    xs = xs.reshape(b, L, SSM_HEADS, SSM_HEADDIM)
    Bm = Bm.reshape(b, L, SSM_GROUPS, D_STATE)
    Cm = Cm.reshape(b, L, SSM_GROUPS, D_STATE)
    dt_dir = jax.nn.softplus(dt.astype(F32).reshape(b, L, 2, SSM_HEADS) + dt_bias.astype(F32))
    A = -jnp.exp(A_log.astype(F32))
    Dk = D_skip.astype(F32)
    y_f, h_f = ssd_scan(xs, dt_dir[:, :, 0], A[0], Bm, Cm, h0[:, 0])
    fl = lambda arr: jnp.flip(arr, axis=1)
    y_b, h_b = ssd_scan(fl(xs), fl(dt_dir[:, :, 1]), A[1], fl(Bm), fl(Cm), h0[:, 1])
    x32 = xs.astype(F32)
    y = (y_f + Dk[0][:, None] * x32) + (fl(y_b) + Dk[1][:, None] * x32)
    y = y.reshape(b, L, D_INNER).astype(u.dtype) * jax.nn.silu(z)
    y = rmsnorm(y, norm_g) @ w_out
    return y, jnp.stack([h_f, h_b], axis=1)


def run_trunk(h, cond, ctx_k, ctx_v, ctx_state, ada_w, ada_b, norm_g, ffn_w_in, ffn_w_out,
              mix_w_in, pool_w, pool_scale, qk_norm_g, mix_w_out,
              ssm_w_in, ssm_conv_w, ssm_conv_b, ssm_dt_bias, ssm_A_log, ssm_D, ssm_norm_g, ssm_w_out):
    is_context = ctx_k is None
    ks, vs, ss = [], [], []
    for l in range(DEPTH):
        mods = jnp.split(jax.nn.silu(cond) @ ada_w[l] + ada_b[l], N_MOD, axis=-1)
        g = norm_g[l]
        u = modulate_in(h, g[0], mods[0], mods[1])
        h = residual_add(h, swiglu(u, ffn_w_in[l, 0], ffn_w_out[l, 0]), g[1], mods[2], 0.5)
        u = modulate_in(h, g[2], mods[3], mods[4])
        if l % 2 == 0:
            e = l // 2
            kv = None if is_context else (ctx_k[:, e], ctx_v[:, e])
            y, k, v = even_mixer(u, mix_w_in[e], pool_w[e], pool_scale[e], qk_norm_g[e], mix_w_out[e], kv)
            ks.append(k)
            vs.append(v)
        else:
            o = l // 2
            if is_context:
                h0 = jnp.zeros((h.shape[0], 2, SSM_HEADS, SSM_HEADDIM, D_STATE), F32)
            else:
                h0 = ctx_state[:, o]
            y, hs = odd_mixer(u, ssm_w_in[o], ssm_conv_w[o], ssm_conv_b[o], ssm_dt_bias[o], ssm_A_log[o],
                              ssm_D[o], ssm_norm_g[o], ssm_w_out[o], h0)
            ss.append(hs)
        h = residual_add(h, y, g[3], mods[5], 1.0)
        u = modulate_in(h, g[4], mods[6], mods[7])
        h = residual_add(h, swiglu(u, ffn_w_in[l, 1], ffn_w_out[l, 1]), g[5], mods[8], 0.5)
    return h, ks, vs, ss


def setup_inputs(seed: int = 0) -> dict:
    key = jax.random.key(seed)
    k = jax.random.split(key, 28)
    nrm = lambda kk, shape, s: jax.random.normal(kk, shape, F32) * s
    u_dt = jax.random.uniform(k[25], (N_ODD, 2, SSM_HEADS), F32)
    dt0 = jnp.exp(u_dt * (math.log(0.1) - math.log(0.001)) + math.log(0.001))
    dt_bias = dt0 + jnp.log(-jnp.expm1(-dt0))
    return {
        "x_prompt": nrm(k[0], (BATCH, SEQ, D_MODEL), 1.0),
        "x_sample": nrm(k[1], (DEC_BATCH, DEC_SEQ, D_MODEL), 1.0),
        "cache_k": nrm(k[2], (DEC_BATCH, N_EVEN, PAST_LEN, N_KV_HEADS, HEAD_DIM), 1.0),
        "cache_v": nrm(k[3], (DEC_BATCH, N_EVEN, PAST_LEN, N_KV_HEADS, HEAD_DIM), 1.0),
        "state_ssm": nrm(k[4], (DEC_BATCH, N_ODD, 2, SSM_HEADS, SSM_HEADDIM, D_STATE), 0.1),
        "c": nrm(k[5], (DEC_BATCH, D_MODEL), 1.0),
        "c_ctx": nrm(k[6], (D_MODEL,), 1.0),
        "ada_w": nrm(k[7], (DEPTH, D_MODEL, N_MOD * D_MODEL), 0.5 * D_MODEL ** -0.5),
        "ada_b": nrm(k[8], (DEPTH, N_MOD * D_MODEL), 0.01),
        "norm_g": 1.0 + nrm(k[9], (DEPTH, 6, D_MODEL), 0.05),
        "ffn_w_in": nrm(k[10], (DEPTH, 2, D_MODEL, 2 * D_FF), D_MODEL ** -0.5),
        "ffn_w_out": nrm(k[11], (DEPTH, 2, D_FF, D_MODEL), D_FF ** -0.5),
        "mix_w_in": nrm(k[12], (N_EVEN, D_MODEL, MIX_IN), D_MODEL ** -0.5),
        "pool_w": nrm(k[13], (N_EVEN, N_POOL_GROUPS, POOL_GROUP_DIM, POOL_GROUP_DIM), POOL_GROUP_DIM ** -0.5),
        "pool_scale": 1.0 + nrm(k[14], (N_EVEN, POOL_WIDTH), 0.1),
        "qk_norm_g": 1.0 + nrm(k[15], (N_EVEN, 2, HEAD_DIM), 0.05),
        "mix_w_out": nrm(k[16], (N_EVEN, MIX_OUT, D_MODEL), MIX_OUT ** -0.5),
        "ssm_w_in": nrm(k[17], (N_ODD, D_MODEL, SSM_IN), D_MODEL ** -0.5),
        "ssm_conv_w": nrm(k[18], (N_ODD, CONV_DIM, D_CONV), D_CONV ** -0.5),
        "ssm_conv_b": nrm(k[19], (N_ODD, CONV_DIM), 0.01),
        "ssm_dt_bias": dt_bias,
        "ssm_A_log": jnp.log(jax.random.uniform(k[20], (N_ODD, 2, SSM_HEADS), F32, 1.0, 16.0)),
        "ssm_D": 1.0 + nrm(k[21], (N_ODD, 2, SSM_HEADS), 0.1),
        "ssm_norm_g": 1.0 + nrm(k[22], (N_ODD, D_INNER), 0.05),
        "ssm_w_out": nrm(k[23], (N_ODD, D_INNER, D_MODEL), D_INNER ** -0.5),
    }


def reference(x_prompt, x_sample, cache_k, cache_v, state_ssm, c, c_ctx,
              ada_w, ada_b, norm_g, ffn_w_in, ffn_w_out,
              mix_w_in, pool_w, pool_scale, qk_norm_g, mix_w_out,
              ssm_w_in, ssm_conv_w, ssm_conv_b, ssm_dt_bias, ssm_A_log, ssm_D, ssm_norm_g, ssm_w_out):
    y_prompt, ks, vs, ss = run_trunk(
        x_prompt, c_ctx[None], None, None, None, ada_w, ada_b, norm_g, ffn_w_in, ffn_w_out,
        mix_w_in, pool_w, pool_scale, qk_norm_g, mix_w_out,
        ssm_w_in, ssm_conv_w, ssm_conv_b, ssm_dt_bias, ssm_A_log, ssm_D, ssm_norm_g, ssm_w_out)
    new_k = jnp.stack(ks, axis=1)
    new_v = jnp.stack(vs, axis=1)
    new_ssm = jnp.stack(ss, axis=1)
    y_sample, _, _, _ = run_trunk(
        x_sample, c, cache_k, cache_v, state_ssm, ada_w, ada_b, norm_g, ffn_w_in, ffn_w_out,
        mix_w_in, pool_w, pool_scale, qk_norm_g, mix_w_out,
        ssm_w_in, ssm_conv_w, ssm_conv_b, ssm_dt_bias, ssm_A_log, ssm_D, ssm_norm_g, ssm_w_out)
    return (y_prompt, y_sample, new_k, new_v, new_ssm)
```

```python
import functools
import math

import jax
import jax.numpy as jnp
from jax import lax
from jax.experimental import pallas as pl
from jax.experimental.pallas import tpu as pltpu

F32 = jnp.float32
BF16 = jnp.bfloat16

D_MODEL = 2048
DEPTH = 2
GRID_W = 64
EPS = 1e-6
N_MOD = 9
D_FF = 5632
POOL_WINDOWS = (2, 4, 8, 16)
POOL_WIDTH = 1024
POOL_GROUP_DIM = 256
HEAD_DIM = 128
N_Q_HEADS = 8
N_KV_HEADS = 2
Q_PER_KV = 4
ATTN_WIDTH = 1024
KV_WIDTH = 256
MIX_IN = 2560
ROPE_THETA = 10000.0
D_INNER = 4096
SSM_HEADDIM = 64
SSM_HEADS = 64
SSM_GROUPS = 8
HEADS_PER_GROUP = 8
D_STATE = 128
SSM_CHUNK = 128
CONV_DIM = 6144
SSM_IN = 10368
GROUP_WIDTH = HEADS_PER_GROUP * SSM_HEADDIM

LANES = 128
VMEM_LIMIT = 56 * 1024 * 1024
COND_ROWS = 16


def _params(sem):
    return pltpu.CompilerParams(dimension_semantics=sem, vmem_limit_bytes=VMEM_LIMIT)


def _rms(x, g):
    ms = jnp.mean(x * x, axis=-1, keepdims=True)
    return x * lax.rsqrt(ms + EPS) * g


def _silu(x):
    return x * jax.nn.sigmoid(x)


def _batch_of_tile(tm, seq, n_cond):
    if n_cond == 1:
        return lambda i: 0
    assert seq % tm == 0
    return lambda i: (i * tm) // seq


def _mods_kernel(cond_ref, w_ref, b_ref, out_ref):
    sc = _silu(cond_ref[...]).astype(BF16)
    w = w_ref[...].astype(BF16)
    out_ref[...] = jnp.dot(sc, w, preferred_element_type=F32) + b_ref[...]


def _ada_mods(cond, ada_w, ada_b):
    tn = 1024
    n_out = N_MOD * D_MODEL
    return pl.pallas_call(
        _mods_kernel,
        grid=(DEPTH, n_out // tn),
        in_specs=[
            pl.BlockSpec((COND_ROWS, D_MODEL), lambda l, n: (0, 0)),
            pl.BlockSpec((None, D_MODEL, tn), lambda l, n: (l, 0, n)),
            pl.BlockSpec((None, 1, tn), lambda l, n: (l, 0, n)),
        ],
        out_specs=pl.BlockSpec((None, COND_ROWS, tn), lambda l, n: (l, 0, n)),
        out_shape=jax.ShapeDtypeStruct((DEPTH, COND_ROWS, n_out), F32),
        compiler_params=_params(("parallel", "parallel")),
        name="ada_mods",
    )(cond, ada_w, ada_b.reshape(DEPTH, 1, n_out))


def _ffn_kernel(h_ref, mods_ref, g_ref, wa_ref, wb_ref, wo_ref, out_ref, u_ref, acc_ref,
                *, mod0, g0):
    j = pl.program_id(1)

    @pl.when(j == 0)
    def _():
        u = _rms(h_ref[...], g_ref[g0:g0 + 1]) * (1 + mods_ref[mod0 + 1:mod0 + 2]) + mods_ref[mod0:mod0 + 1]
        u_ref[...] = u.astype(BF16)
        acc_ref[...] = jnp.zeros_like(acc_ref)

    u = u_ref[...]
    a = jnp.dot(u, wa_ref[...], preferred_element_type=F32)
    b = jnp.dot(u, wb_ref[...], preferred_element_type=F32)
    hidden = (_silu(a) * b).astype(BF16)
    acc_ref[...] += jnp.dot(hidden, wo_ref[...], preferred_element_type=F32)

    @pl.when(j == pl.num_programs(1) - 1)
    def _():
        r = _rms(acc_ref[...], g_ref[g0 + 1:g0 + 2])
        out_ref[...] = h_ref[...] + 0.5 * mods_ref[mod0 + 2:mod0 + 3] * r


def _ffn(h, mods, g, w_in, w_out, *, mod0, g0, seq):
    tokens = h.shape[0]
    tm, tf = 512, 512
    nf = D_FF // tf
    bidx = _batch_of_tile(tm, seq, mods.shape[0])
    return pl.pallas_call(
        functools.partial(_ffn_kernel, mod0=mod0, g0=g0),
        grid=(tokens // tm, nf),
        in_specs=[
            pl.BlockSpec((tm, D_MODEL), lambda i, j: (i, 0)),
            pl.BlockSpec((None, N_MOD, D_MODEL), lambda i, j: (bidx(i), 0, 0)),
            pl.BlockSpec((6, D_MODEL), lambda i, j: (0, 0)),
            pl.BlockSpec((D_MODEL, tf), lambda i, j: (0, j)),
            pl.BlockSpec((D_MODEL, tf), lambda i, j: (0, nf + j)),
            pl.BlockSpec((tf, D_MODEL), lambda i, j: (j, 0)),
        ],
        out_specs=pl.BlockSpec((tm, D_MODEL), lambda i, j: (i, 0)),
        out_shape=jax.ShapeDtypeStruct((tokens, D_MODEL), F32),
        scratch_shapes=[pltpu.VMEM((tm, D_MODEL), BF16), pltpu.VMEM((tm, D_MODEL), F32)],
        compiler_params=_params(("parallel", "arbitrary")),
        name="ffn",
    )(h, mods, g, w_in, w_in, w_out)


def _shift_rows(x, d, pos, seq):
    n = x.shape[0]
    y = pltpu.roll(x, (-d) % n, axis=0)
    valid = (pos + d >= 0) & (pos + d < seq)
    return jnp.where(valid, y, 0.0)


def _pool_delta(x, window, pos, seq):
    half = window // 2
    fwd = x
    bwd = x
    m = 1
    while m < half:
        fwd = fwd + _shift_rows(fwd, m, pos, seq)
        bwd = bwd + _shift_rows(bwd, -m, pos, seq)
        m *= 2
    total = fwd + _shift_rows(bwd, -1, pos, seq)
    count = jnp.minimum(pos + half, seq) - jnp.maximum(pos - half, 0)
    return total / count.astype(F32) - x


def _head_norm(x, g):
    ms = jnp.mean(x * x, axis=-1, keepdims=True)
    return x * lax.rsqrt(ms + EPS) * g


def _rope(x, cos, sin_lo, sin_hi):
    quarter = HEAD_DIM // 4
    return (x * cos + pltpu.roll(x, HEAD_DIM - quarter, axis=1) * sin_lo
            + pltpu.roll(x, quarter, axis=1) * sin_hi)


def _even_in_kernel(h_ref, mods_ref, g_ref, w_ref, pw_ref, ps_ref, qkg_ref, *rest, seq, rope):
    if rope:
        cos_ref, slo_ref, shi_ref, pool_ref, q_ref, k_ref, v_ref, u_ref = rest
    else:
        pool_ref, q_ref, k_ref, v_ref, u_ref = rest
    j = pl.program_id(1)
    tm = h_ref.shape[0]

    @pl.when(j == 0)
    def _():
        u = _rms(h_ref[...], g_ref[2:3]) * (1 + mods_ref[4:5]) + mods_ref[3:4]
        u_ref[...] = u.astype(BF16)

    p = jnp.dot(u_ref[...], w_ref[...], preferred_element_type=F32)

    def rot(x):
        if rope:
            return _rope(x, cos_ref[...], slo_ref[...], shi_ref[...])
        return x

    for jj in range(2):
        @pl.when(j == jj)
        def _(jj=jj):
            pos = lax.broadcasted_iota(jnp.int32, (tm, 1), 0) % seq
            for gi in range(2):
                sl = slice(gi * POOL_GROUP_DIM, (gi + 1) * POOL_GROUP_DIM)
                d = _pool_delta(p[:, sl], POOL_WINDOWS[2 * jj + gi], pos, seq)
                y = jnp.dot(d.astype(BF16), pw_ref[gi], preferred_element_type=F32)
                pool_ref[:, sl] = (y * ps_ref[:, sl]).astype(BF16)

    @pl.when((j == 2) | (j == 3))
    def _():
        for hd in range(4):
            sl = slice(hd * HEAD_DIM, (hd + 1) * HEAD_DIM)
            q_ref[:, sl] = rot(_head_norm(p[:, sl], qkg_ref[0:1])).astype(BF16)

    @pl.when(j == 4)
    def _():
        for hd in range(N_KV_HEADS):
            sl = slice(hd * HEAD_DIM, (hd + 1) * HEAD_DIM)
            k_ref[:, sl] = rot(_head_norm(p[:, sl], qkg_ref[1:2]))
        v_ref[...] = p[:, KV_WIDTH:]


def _rope_tables(seq):
    pos = jnp.arange(seq)
    row = (pos // GRID_W).astype(F32)
    col = (pos % GRID_W).astype(F32)
    quarter = HEAD_DIM // 4
    inv_freq = ROPE_THETA ** (-jnp.arange(quarter, dtype=F32) / quarter)
    ang_r = row[:, None] * inv_freq[None]
    ang_c = col[:, None] * inv_freq[None]
    zero = jnp.zeros_like(ang_r)
    cos = jnp.concatenate([jnp.cos(ang_r), jnp.cos(ang_r), jnp.cos(ang_c), jnp.cos(ang_c)], axis=-1)
    sin_lo = jnp.concatenate([-jnp.sin(ang_r), zero, -jnp.sin(ang_c), zero], axis=-1)
    sin_hi = jnp.concatenate([zero, jnp.sin(ang_r), zero, jnp.sin(ang_c)], axis=-1)
    return cos, sin_lo, sin_hi


def _even_in(h, mods, g, w_in, pool_w, pool_scale, qk_g, *, seq, rope):
    tokens = h.shape[0]
    tm, tn = 1024, 512
    assert tm % seq == 0 and tokens % tm == 0
    bidx = _batch_of_tile(tm, seq, mods.shape[0])
    n_pool = POOL_WIDTH // tn
    n_q = ATTN_WIDTH // tn
    in_specs = [
        pl.BlockSpec((tm, D_MODEL), lambda i, j: (i, 0)),
        pl.BlockSpec((None, N_MOD, D_MODEL), lambda i, j: (bidx(i), 0, 0)),
        pl.BlockSpec((6, D_MODEL), lambda i, j: (0, 0)),
        pl.BlockSpec((D_MODEL, tn), lambda i, j: (0, j)),
        pl.BlockSpec((2, POOL_GROUP_DIM, POOL_GROUP_DIM), lambda i, j: (jnp.minimum(j, n_pool - 1), 0, 0)),
        pl.BlockSpec((1, tn), lambda i, j: (0, jnp.minimum(j, n_pool - 1))),
        pl.BlockSpec((2, HEAD_DIM), lambda i, j: (0, 0)),
    ]
    args = [h, mods, g, w_in, pool_w, pool_scale, qk_g]
    if rope:
        assert tm == seq
        in_specs += [pl.BlockSpec((tm, HEAD_DIM), lambda i, j: (0, 0))] * 3
        args += list(_rope_tables(seq))
    return pl.pallas_call(
        functools.partial(_even_in_kernel, seq=seq, rope=rope),
        grid=(tokens // tm, MIX_IN // tn),
        in_specs=in_specs,
        out_specs=[
            pl.BlockSpec((tm, tn), lambda i, j: (i, jnp.minimum(j, n_pool - 1))),
            pl.BlockSpec((tm, tn), lambda i, j: (i, jnp.clip(j - n_pool, 0, n_q - 1))),
            pl.BlockSpec((tm, KV_WIDTH), lambda i, j: (i, 0)),
            pl.BlockSpec((tm, KV_WIDTH), lambda i, j: (i, 0)),
        ],
        out_shape=[
            jax.ShapeDtypeStruct((tokens, POOL_WIDTH), BF16),
            jax.ShapeDtypeStruct((tokens, ATTN_WIDTH), BF16),
            jax.ShapeDtypeStruct((tokens, KV_WIDTH), F32),
            jax.ShapeDtypeStruct((tokens, KV_WIDTH), F32),
        ],
        scratch_shapes=[pltpu.VMEM((tm, D_MODEL), BF16)],
        compiler_params=_params(("parallel", "arbitrary")),
        name="even_in",
    )(*args)


def _attn_kernel(q_ref, k_ref, v_ref, *rest, cached):
    if cached:
        ck_ref, cv_ref, o_ref = rest
    else:
        (o_ref,) = rest
    scale = HEAD_DIM ** -0.5
    nt = (((1,), (1,)), ((), ()))
    kb = k_ref[...].astype(BF16)
    vb = v_ref[...].astype(BF16)
    if cached:
        ckb = ck_ref[...].astype(BF16)
        cvb = cv_ref[...].astype(BF16)
    for hd in range(Q_PER_KV):
        sl = slice(hd * HEAD_DIM, (hd + 1) * HEAD_DIM)
        q = q_ref[:, sl]
        s = lax.dot_general(q, kb, nt, preferred_element_type=F32) * scale
        m = jnp.max(s, axis=-1, keepdims=True)
        if cached:
            sc = lax.dot_general(q, ckb, nt, preferred_element_type=F32) * scale
            m = jnp.maximum(m, jnp.max(sc, axis=-1, keepdims=True))
        e = jnp.exp(s - m)
        denom = jnp.sum(e, axis=-1, keepdims=True)
        o = jnp.dot(e.astype(BF16), vb, preferred_element_type=F32)
        if cached:
            ec = jnp.exp(sc - m)
            denom = denom + jnp.sum(ec, axis=-1, keepdims=True)
            o = o + jnp.dot(ec.astype(BF16), cvb, preferred_element_type=F32)
        o_ref[:, sl] = (o / denom).astype(BF16)


def _attention(q, k, v, cache_k, cache_v, *, batch, seq):
    tokens = q.shape[0]
    tq = 256
    nq = seq // tq
    gw = Q_PER_KV * HEAD_DIM
    cached = cache_k is not None
    in_specs = [
        pl.BlockSpec((tq, gw), lambda b, g, i: (b * nq + i, g)),
        pl.BlockSpec((seq, HEAD_DIM), lambda b, g, i: (b, g)),
        pl.BlockSpec((seq, HEAD_DIM), lambda b, g, i: (b, g)),
    ]
    args = [q, k, v]
    if cached:
        past = cache_k.shape[1]
        in_specs += [pl.BlockSpec((None, past, HEAD_DIM), lambda b, g, i: (b, 0, g))] * 2
        args += [cache_k, cache_v]
    return pl.pallas_call(
        functools.partial(_attn_kernel, cached=cached),
        grid=(batch, N_KV_HEADS, nq),
        in_specs=in_specs,
        out_specs=pl.BlockSpec((tq, gw), lambda b, g, i: (b * nq + i, g)),
        out_shape=jax.ShapeDtypeStruct((tokens, ATTN_WIDTH), BF16),
        compiler_params=_params(("parallel", "parallel", "parallel")),
        name="attention",
    )(*args)


def _even_out_kernel(pool_ref, attn_ref, w_ref, h_ref, mods_ref, g_ref, out_ref):
    y = jnp.dot(pool_ref[...], w_ref[:POOL_WIDTH], preferred_element_type=F32)
    y = y + jnp.dot(attn_ref[...], w_ref[POOL_WIDTH:], preferred_element_type=F32)
    out_ref[...] = h_ref[...] + mods_ref[5:6] * _rms(y, g_ref[3:4])


def _even_out(pool_y, attn, w_out, h, mods, g, *, seq):
    tokens = h.shape[0]
    tm = 512
    bidx = _batch_of_tile(tm, seq, mods.shape[0])
    return pl.pallas_call(
        _even_out_kernel,
        grid=(tokens // tm,),
        in_specs=[
            pl.BlockSpec((tm, POOL_WIDTH), lambda i: (i, 0)),
            pl.BlockSpec((tm, ATTN_WIDTH), lambda i: (i, 0)),
            pl.BlockSpec((POOL_WIDTH + ATTN_WIDTH, D_MODEL), lambda i: (0, 0)),
            pl.BlockSpec((tm, D_MODEL), lambda i: (i, 0)),
            pl.BlockSpec((None, N_MOD, D_MODEL), lambda i: (bidx(i), 0, 0)),
            pl.BlockSpec((6, D_MODEL), lambda i: (0, 0)),
        ],
        out_specs=pl.BlockSpec((tm, D_MODEL), lambda i: (i, 0)),
        out_shape=jax.ShapeDtypeStruct((tokens, D_MODEL), F32),
        compiler_params=_params(("parallel",)),
        name="even_out",
    )(pool_y, attn, w_out, h, mods, g)


def _odd_out_kernel(x_ref, w_ref, h_ref, mods_ref, g_ref, out_ref, acc_ref):
    k = pl.program_id(1)

    @pl.when(k == 0)
    def _():
        acc_ref[...] = jnp.zeros_like(acc_ref)

    acc_ref[...] += jnp.dot(x_ref[...], w_ref[...], preferred_element_type=F32)

    @pl.when(k == pl.num_programs(1) - 1)
    def _():
        out_ref[...] = h_ref[...] + mods_ref[5:6] * _rms(acc_ref[...], g_ref[3:4])


def _odd_out(yn, w_out, h, mods, g, *, seq):
    tokens = h.shape[0]
    tm = 512
    bidx = _batch_of_tile(tm, seq, mods.shape[0])
    return pl.pallas_call(
        _odd_out_kernel,
        grid=(tokens // tm, SSM_GROUPS),
        in_specs=[
            pl.BlockSpec((None, tm, GROUP_WIDTH), lambda i, k: (k, i, 0)),
            pl.BlockSpec((GROUP_WIDTH, D_MODEL), lambda i, k: (k, 0)),
            pl.BlockSpec((tm, D_MODEL), lambda i, k: (i, 0)),
            pl.BlockSpec((None, N_MOD, D_MODEL), lambda i, k: (bidx(i), 0, 0)),
            pl.BlockSpec((6, D_MODEL), lambda i, k: (0, 0)),
        ],
        out_specs=pl.BlockSpec((tm, D_MODEL), lambda i, k: (i, 0)),
        out_shape=jax.ShapeDtypeStruct((tokens, D_MODEL), F32),
        scratch_shapes=[pltpu.VMEM((tm, D_MODEL), F32)],
        compiler_params=_params(("parallel", "arbitrary")),
        name="odd_out",
    )(yn, w_out, h, mods, g)


N_Z = D_INNER // GROUP_WIDTH
N_X = D_INNER // GROUP_WIDTH
N_BC = SSM_GROUPS * D_STATE // GROUP_WIDTH
GROUPS_PER_CHUNK = GROUP_WIDTH // D_STATE


def _odd_in_kernel(h_ref, mods_ref, g_ref, w_ref, wdt_ref, cw_ref, cb_ref, dtb_ref,
                   z_ref, x_ref, b_ref, c_ref, dt_ref, u_ref, *, seq):
    j = pl.program_id(1)
    tm = h_ref.shape[0]

    @pl.when(j == 0)
    def _():
        u = _rms(h_ref[...], g_ref[2:3]) * (1 + mods_ref[4:5]) + mods_ref[3:4]
        u_ref[...] = u.astype(BF16)
        raw = jnp.dot(u_ref[...], wdt_ref[...], preferred_element_type=F32) + dtb_ref[...]
        dt_ref[...] = jnp.maximum(raw, 0.0) + jnp.log1p(jnp.exp(-jnp.abs(raw)))

    p = jnp.dot(u_ref[...], w_ref[...], preferred_element_type=F32)

    @pl.when(j < N_Z)
    def _():
        z_ref[...] = p

    def conv_silu():
        pos = lax.broadcasted_iota(jnp.int32, (tm, 1), 0) % seq
        y = (_shift_rows(p, -1, pos, seq) * cw_ref[0:1] + p * cw_ref[1:2]
             + _shift_rows(p, 1, pos, seq) * cw_ref[2:3] + cb_ref[...])
        return _silu(y)

    @pl.when((j >= N_Z) & (j < N_Z + N_X))
    def _():
        x_ref[...] = conv_silu()

    @pl.when((j >= N_Z + N_X) & (j < N_Z + N_X + N_BC))
    def _():
        y = conv_silu().astype(BF16)
        for gi in range(GROUPS_PER_CHUNK):
            b_ref[gi] = y[:, gi * D_STATE:(gi + 1) * D_STATE]

    @pl.when(j >= N_Z + N_X + N_BC)
    def _():
        y = conv_silu().astype(BF16)
        for gi in range(GROUPS_PER_CHUNK):
            c_ref[gi] = y[:, gi * D_STATE:(gi + 1) * D_STATE]


def _odd_in(h, mods, g, w_in, conv_w, conv_b, dt_bias, *, seq):
    tokens = h.shape[0]
    tm, tn = 1024, GROUP_WIDTH
    assert tm % seq == 0 and tokens % tm == 0
    bidx = _batch_of_tile(tm, seq, mods.shape[0])
    n_main = (D_INNER + CONV_DIM) // tn
    n_conv = CONV_DIM // tn
    dt_block = (D_INNER + CONV_DIM) // LANES
    conv_idx = lambda j: jnp.clip(j - N_Z, 0, n_conv - 1)
    return pl.pallas_call(
        functools.partial(_odd_in_kernel, seq=seq),
        grid=(tokens // tm, n_main),
        in_specs=[
            pl.BlockSpec((tm, D_MODEL), lambda i, j: (i, 0)),
            pl.BlockSpec((None, N_MOD, D_MODEL), lambda i, j: (bidx(i), 0, 0)),
            pl.BlockSpec((6, D_MODEL), lambda i, j: (0, 0)),
            pl.BlockSpec((D_MODEL, tn), lambda i, j: (0, j)),
            pl.BlockSpec((D_MODEL, 2 * SSM_HEADS), lambda i, j: (0, dt_block)),
            pl.BlockSpec((3, tn), lambda i, j: (0, conv_idx(j))),
            pl.BlockSpec((1, tn), lambda i, j: (0, conv_idx(j))),
            pl.BlockSpec((1, 2 * SSM_HEADS), lambda i, j: (0, 0)),
        ],
        out_specs=[
            pl.BlockSpec((None, tm, tn), lambda i, j: (jnp.minimum(j, N_Z - 1), i, 0)),
            pl.BlockSpec((None, tm, tn), lambda i, j: (jnp.clip(j - N_Z, 0, N_X - 1), i, 0)),
            pl.BlockSpec((GROUPS_PER_CHUNK, tm, D_STATE),
                         lambda i, j: (jnp.clip(j - N_Z - N_X, 0, N_BC - 1), i, 0)),
            pl.BlockSpec((GROUPS_PER_CHUNK, tm, D_STATE),
                         lambda i, j: (jnp.clip(j - N_Z - N_X - N_BC, 0, N_BC - 1), i, 0)),
            pl.BlockSpec((tm, 2 * SSM_HEADS), lambda i, j: (i, 0)),
        ],
        out_shape=[
            jax.ShapeDtypeStruct((N_Z, tokens, tn), F32),
            jax.ShapeDtypeStruct((N_X, tokens, tn), F32),
            jax.ShapeDtypeStruct((SSM_GROUPS, tokens, D_STATE), BF16),
            jax.ShapeDtypeStruct((SSM_GROUPS, tokens, D_STATE), BF16),
            jax.ShapeDtypeStruct((tokens, 2 * SSM_HEADS), F32),
        ],
        scratch_shapes=[pltpu.VMEM((tm, D_MODEL), BF16)],
        compiler_params=_params(("parallel", "arbitrary")),
        name="odd_in",
    )(h, mods, g, w_in, w_in, conv_w, conv_b, dt_bias)


def _split3(x):
    x1 = x.astype(BF16)
    r1 = x - x1.astype(F32)
    x2 = r1.astype(BF16)
    x3 = (r1 - x2.astype(F32)).astype(BF16)
    return x1, x2, x3


def _ssd_kernel(*refs, direction, has_h0, emit_state, combine):
    refs = list(refs)
    x_ref, b_ref, c_ref, dt_ref, alog_ref = refs[:5]
    del refs[:5]
    h0_ref = refs.pop(0) if has_h0 else None
    if combine:
        yf_ref, z_ref, dskip_ref, ng_ref = refs[:4]
        del refs[:4]
    y_ref = refs.pop(0)
    hT_out_ref = refs.pop(0) if emit_state else None
    ht_ref = refs.pop(0)
    yb_ref = refs.pop(0) if combine else None

    c = pl.program_id(1)
    q = SSM_CHUNK
    pairs = GROUP_WIDTH // LANES

    @pl.when(c == 0)
    def _():
        if has_h0:
            for g in range(SSM_GROUPS):
                for pr in range(pairs):
                    r0 = g * GROUP_WIDTH + pr * LANES
                    ht_ref[g, :, pr * LANES:(pr + 1) * LANES] = h0_ref[r0:r0 + LANES, :].T
        else:
            ht_ref[...] = jnp.zeros_like(ht_ref)

    ii = lax.broadcasted_iota(jnp.int32, (q, q), 0)
    jj = lax.broadcasted_iota(jnp.int32, (q, q), 1)
    causal = (jj <= ii) if direction == 0 else (jj >= ii)
    tri = jnp.where(causal, 1.0, 0.0).astype(BF16)
    low_lane = lax.broadcasted_iota(jnp.int32, (1, LANES), 1) < SSM_HEADDIM

    dt = dt_ref[...]
    dta = dt * (-jnp.exp(alog_ref[...]))
    a = sum(jnp.dot(tri, part, preferred_element_type=F32) for part in _split3(dta))
    a_end = a[q - 1:q] if direction == 0 else a[0:1]
    a_t = a.T
    dt_t = dt.T
    s_t = (dt * jnp.exp(a_end - a)).T
    end_scale = jnp.exp(a_end)

    for g in range(SSM_GROUPS):
        bg_t = b_ref[g].astype(F32).T
        cg = c_ref[g]
        cg32 = cg.astype(F32)
        cb = jnp.dot(cg, bg_t.astype(BF16), preferred_element_type=F32)
        for pr in range(pairs):
            lanes = slice(pr * LANES, (pr + 1) * LANES)
            xp = x_ref[g, :, lanes].astype(BF16)
            hp = ht_ref[g, :, lanes]
            hpb = hp.astype(BF16)
            ys, ds, scales = [], [], []
            for e in range(2):
                hl = direction * SSM_HEADS + g * HEADS_PER_GROUP + 2 * pr + e
                col = jnp.broadcast_to(a[:, hl:hl + 1], (q, q))
                decay = jnp.exp(jnp.where(causal, col - a_t[hl:hl + 1, :], -jnp.inf))
                w = (cb * decay * dt_t[hl:hl + 1, :]).astype(BF16)
                cdec = (cg32 * jnp.exp(col)).astype(BF16)
                ys.append(jnp.dot(w, xp, preferred_element_type=F32)
                          + jnp.dot(cdec, hpb, preferred_element_type=F32))
                bs = (bg_t * s_t[hl:hl + 1, :]).astype(BF16)
                ds.append(jnp.dot(bs, xp, preferred_element_type=F32))
                scales.append(jnp.broadcast_to(end_scale[:, hl:hl + 1], (1, LANES)))
            y_pair = jnp.where(low_lane, ys[0], ys[1])
            if combine:
                yb_ref[g, :, lanes] = y_pair
            else:
                y_ref[g, :, lanes] = y_pair
            ht_ref[g, :, lanes] = (hp * jnp.where(low_lane, scales[0], scales[1])
                                   + jnp.where(low_lane, ds[0], ds[1]))

    if combine:
        ssq = jnp.zeros((q, 1), F32)
        for g in range(SSM_GROUPS):
            xg = x_ref[g]
            y = (yf_ref[g] + dskip_ref[0, g] * xg) + (yb_ref[g] + dskip_ref[1, g] * xg)
            y = y * _silu(z_ref[g])
            yb_ref[g] = y
            ssq = ssq + jnp.sum(y * y, axis=-1, keepdims=True)
        inv = lax.rsqrt(ssq / D_INNER + EPS)
        for g in range(SSM_GROUPS):
            y_ref[g] = (yb_ref[g] * inv * ng_ref[g]).astype(BF16)

    if emit_state:
        @pl.when(c == pl.num_programs(1) - 1)
        def _():
            for g in range(SSM_GROUPS):
                for pr in range(pairs):
                    r0 = g * GROUP_WIDTH + pr * LANES
                    hT_out_ref[r0:r0 + LANES, :] = ht_ref[g, :, pr * LANES:(pr + 1) * LANES].T


def _ssd(xs, bm, cm, dt, a_log, h0, *, direction, batch, seq, emit_state, combine=None):
    tokens = xs.shape[1]
    q = SSM_CHUNK
    nc = seq // q
    if direction == 0:
        chunk = lambda b, c: b * nc + c
    else:
        chunk = lambda b, c: b * nc + (nc - 1 - c)
    grp = lambda width: pl.BlockSpec((SSM_GROUPS, q, width), lambda b, c: (0, chunk(b, c), 0))
    in_specs = [grp(GROUP_WIDTH), grp(D_STATE), grp(D_STATE),
                pl.BlockSpec((q, 2 * SSM_HEADS), lambda b, c: (chunk(b, c), 0)),
                pl.BlockSpec((1, 2 * SSM_HEADS), lambda b, c: (0, 0))]
    args = [xs, bm, cm, dt, a_log]
    if h0 is not None:
        in_specs.append(pl.BlockSpec((None, None, D_INNER, D_STATE), lambda b, c: (b, direction, 0, 0)))
        args.append(h0)
    if combine is not None:
        y_f, z, d_skip, norm_g = combine
        in_specs += [grp(GROUP_WIDTH), grp(GROUP_WIDTH),
                     pl.BlockSpec((2, SSM_GROUPS, 1, GROUP_WIDTH), lambda b, c: (0, 0, 0, 0)),
                     pl.BlockSpec((SSM_GROUPS, 1, GROUP_WIDTH), lambda b, c: (0, 0, 0))]
        args += [y_f, z, d_skip, norm_g]
    out_specs = [grp(GROUP_WIDTH)]
    out_shape = [jax.ShapeDtypeStruct((SSM_GROUPS, tokens, GROUP_WIDTH), BF16 if combine is not None else F32)]
    if emit_state:
        out_specs.append(pl.BlockSpec((None, D_INNER, D_STATE), lambda b, c: (b, 0, 0)))
        out_shape.append(jax.ShapeDtypeStruct((batch, D_INNER, D_STATE), F32))
    scratch = [pltpu.VMEM((SSM_GROUPS, D_STATE, GROUP_WIDTH), F32)]
    if combine is not None:
        scratch.append(pltpu.VMEM((SSM_GROUPS, q, GROUP_WIDTH), F32))
    return pl.pallas_call(
        functools.partial(_ssd_kernel, direction=direction, has_h0=h0 is not None,
                          emit_state=emit_state, combine=combine is not None),
        grid=(batch, nc),
        in_specs=in_specs,
        out_specs=out_specs,
        out_shape=out_shape,
        scratch_shapes=scratch,
        compiler_params=_params(("parallel", "arbitrary")),
        name="ssd_fwd" if direction == 0 else "ssd_bwd",
    )(*args)


def _trunk(h, mods, weights, *, batch, seq, cache_k=None, cache_v=None, state=None):
    context = cache_k is None
    outs = {}
    for l in range(DEPTH):
        g = weights["norm_g"][l]
        m = mods[l]
        h = _ffn(h, m, g, weights["ffn_w_in"][l, 0], weights["ffn_w_out"][l, 0], mod0=0, g0=0, seq=seq)
        if l % 2 == 0:
            e = l // 2
            pool_y, q, k, v = _even_in(h, m, g, weights["mix_w_in"][e], weights["pool_w"][e],
                                       weights["pool_scale"][e], weights["qk_norm_g"][e],
                                       seq=seq, rope=not context)
            ck = None if context else cache_k[:, e]
            cv = None if context else cache_v[:, e]
            attn = _attention(q, k, v, ck, cv, batch=batch, seq=seq)
            h = _even_out(pool_y, attn, weights["mix_w_out"][e], h, m, g, seq=seq)
            outs["k"], outs["v"] = k, v
        else:
            o = l // 2
            z, xs, bm, cm, dt = _odd_in(h, m, g, weights["ssm_w_in"][o], weights["ssm_conv_w"][o],
                                        weights["ssm_conv_b"][o], weights["ssm_dt_bias"][o], seq=seq)
            h0 = None if context else state[:, o]
            a_log = weights["ssm_A_log"][o]
            fwd = _ssd(xs, bm, cm, dt, a_log, h0, direction=0, batch=batch, seq=seq, emit_state=context)
            bwd = _ssd(xs, bm, cm, dt, a_log, h0, direction=1, batch=batch, seq=seq, emit_state=context,
                       combine=(fwd[0], z, weights["ssm_D"][o], weights["ssm_norm_g"][o]))
            if context:
                outs["ssm"] = jnp.stack([fwd[1], bwd[1]], axis=1)
            h = _odd_out(bwd[0], weights["ssm_w_out"][o], h, m, g, seq=seq)
        h = _ffn(h, m, g, weights["ffn_w_in"][l, 1], weights["ffn_w_out"][l, 1], mod0=6, g0=4, seq=seq)
    return h, outs


def kernel(x_prompt, x_sample, cache_k, cache_v, state_ssm, c, c_ctx, ada_w, ada_b, norm_g, ffn_w_in,
           ffn_w_out, mix_w_in, pool_w, pool_scale, qk_norm_g, mix_w_out, ssm_w_in, ssm_conv_w,
           ssm_conv_b, ssm_dt_bias, ssm_A_log, ssm_D, ssm_norm_g, ssm_w_out):
    batch, seq, _ = x_prompt.shape
    dec_batch, dec_seq, _ = x_sample.shape
    n_even, n_odd = mix_w_in.shape[0], ssm_w_in.shape[0]
    past = cache_k.shape[2]

    cond = jnp.zeros((COND_ROWS, D_MODEL), F32).at[0].set(c_ctx).at[1:1 + dec_batch].set(c)
    mods = _ada_mods(cond, ada_w, ada_b).reshape(DEPTH, COND_ROWS, N_MOD, D_MODEL)

    weights = dict(
        norm_g=norm_g,
        ffn_w_in=ffn_w_in.astype(BF16), ffn_w_out=ffn_w_out.astype(BF16),
        mix_w_in=mix_w_in.astype(BF16), pool_w=pool_w.astype(BF16),
        pool_scale=pool_scale.reshape(n_even, 1, POOL_WIDTH), qk_norm_g=qk_norm_g,
        mix_w_out=mix_w_out.astype(BF16),
        ssm_w_in=ssm_w_in.astype(BF16),
        ssm_conv_w=jnp.swapaxes(ssm_conv_w, 1, 2), ssm_conv_b=ssm_conv_b.reshape(n_odd, 1, CONV_DIM),
        ssm_dt_bias=ssm_dt_bias.reshape(n_odd, 1, 2 * SSM_HEADS),
        ssm_A_log=ssm_A_log.reshape(n_odd, 1, 2 * SSM_HEADS),
        ssm_D=jnp.repeat(ssm_D, SSM_HEADDIM, axis=-1).reshape(n_odd, 2, SSM_GROUPS, 1, GROUP_WIDTH),
        ssm_norm_g=ssm_norm_g.reshape(n_odd, SSM_GROUPS, 1, GROUP_WIDTH),
        ssm_w_out=ssm_w_out.astype(BF16),
    )

    y_prompt, ctx = _trunk(x_prompt.reshape(batch * seq, D_MODEL), mods[:, 0:1], weights,
                           batch=batch, seq=seq)
    y_sample, _ = _trunk(x_sample.reshape(dec_batch * dec_seq, D_MODEL), mods[:, 1:1 + dec_batch], weights,
                         batch=dec_batch, seq=dec_seq,
                         cache_k=cache_k.reshape(dec_batch, n_even, past, KV_WIDTH),
                         cache_v=cache_v.reshape(dec_batch, n_even, past, KV_WIDTH),
                         state=state_ssm.reshape(dec_batch, n_odd, 2, D_INNER, D_STATE))

    new_k = ctx["k"].reshape(batch, n_even, seq, N_KV_HEADS, HEAD_DIM)
    new_v = ctx["v"].reshape(batch, n_even, seq, N_KV_HEADS, HEAD_DIM)
    new_ssm = ctx["ssm"].reshape(batch, n_odd, 2, SSM_HEADS, SSM_HEADDIM, D_STATE)
    return (y_prompt.reshape(batch, seq, D_MODEL), y_sample.reshape(dec_batch, dec_seq, D_MODEL),
            new_k, new_v, new_ssm)
```

```python
import functools

import jax
import jax.numpy as jnp
from jax import lax
from jax.experimental import pallas as pl
from jax.experimental.pallas import tpu as pltpu

F32 = jnp.float32
BF16 = jnp.bfloat16

D_MODEL = 2048
DEPTH = 2
GRID_W = 64
EPS = 1e-6
N_MOD = 9
N_NORM = 6
D_FF = 5632
POOL_WINDOWS = (2, 4, 8, 16)
POOL_WIDTH = 1024
POOL_GROUP_DIM = 256
HEAD_DIM = 128
N_KV_HEADS = 2
Q_PER_KV = 4
ATTN_WIDTH = 1024
KV_WIDTH = 256
MIX_IN = 2560
ROPE_THETA = 10000.0
D_INNER = 4096
SSM_HEADDIM = 64
SSM_HEADS = 64
SSM_GROUPS = 8
HEADS_PER_GROUP = 8
D_STATE = 128
SSM_CHUNK = 128
CONV_DIM = 6144
GROUP_WIDTH = HEADS_PER_GROUP * SSM_HEADDIM

LANES = 128
VMEM_LIMIT = 56 * 1024 * 1024
COND_ROWS = 16


def _params(sem):
    return pltpu.CompilerParams(dimension_semantics=sem, vmem_limit_bytes=VMEM_LIMIT)


def _rms(x, g):
    ms = jnp.mean(x * x, axis=-1, keepdims=True)
    return x * lax.rsqrt(ms + EPS) * g


def _silu(x):
    return x * jax.nn.sigmoid(x)


class _Cond:
    def __init__(self, layer, row0, n_cond, seq):
        self.layer, self.row0, self.n_cond, self.seq = layer, row0, n_cond, seq

    def specs(self, tm):
        layer, row0 = self.layer, self.row0
        if self.n_cond == 1:
            row = lambda i: row0
        else:
            assert self.seq % tm == 0
            per_seq = self.seq // tm
            row = lambda i: row0 + i // per_seq
        return [pl.BlockSpec((None, None, N_MOD, D_MODEL), lambda i, *_: (layer, row(i), 0, 0)),
                pl.BlockSpec((None, N_NORM, D_MODEL), lambda *_: (layer, 0, 0))]


def _mods_kernel(cond_ref, w_ref, b_ref, out_ref):
    sc = _silu(cond_ref[...]).astype(BF16)
    w = w_ref[...].astype(BF16)
    out_ref[...] = jnp.dot(sc, w, preferred_element_type=F32) + b_ref[...]


def _ada_mods(cond, ada_w, ada_b):
    tn = 1024
    n_out = N_MOD * D_MODEL
    return pl.pallas_call(
        _mods_kernel,
        grid=(DEPTH, n_out // tn),
        in_specs=[
            pl.BlockSpec((COND_ROWS, D_MODEL), lambda l, n: (0, 0)),
            pl.BlockSpec((None, D_MODEL, tn), lambda l, n: (l, 0, n)),
            pl.BlockSpec((None, 1, tn), lambda l, n: (l, 0, n)),
        ],
        out_specs=pl.BlockSpec((None, COND_ROWS, tn), lambda l, n: (l, 0, n)),
        out_shape=jax.ShapeDtypeStruct((DEPTH, COND_ROWS, n_out), F32),
        compiler_params=_params(("parallel", "parallel")),
        name="ada_mods",
    )(cond, ada_w, ada_b.reshape(DEPTH, 1, n_out))


def _ffn_kernel(h_ref, mods_ref, g_ref, wa_ref, wb_ref, wo_ref, out_ref, u_ref, acc_ref,
                *, mod0, g0):
    j = pl.program_id(1)

    @pl.when(j == 0)
    def _():
        u = _rms(h_ref[...], g_ref[g0:g0 + 1]) * (1 + mods_ref[mod0 + 1:mod0 + 2]) + mods_ref[mod0:mod0 + 1]
        u_ref[...] = u.astype(BF16)
        acc_ref[...] = jnp.zeros_like(acc_ref)

    u = u_ref[...]
    a = jnp.dot(u, wa_ref[...], preferred_element_type=F32)
    b = jnp.dot(u, wb_ref[...], preferred_element_type=F32)
    hidden = (_silu(a) * b).astype(BF16)
    acc_ref[...] += jnp.dot(hidden, wo_ref[...], preferred_element_type=F32)

    @pl.when(j == pl.num_programs(1) - 1)
    def _():
        r = _rms(acc_ref[...], g_ref[g0 + 1:g0 + 2])
        out_ref[...] = h_ref[...] + 0.5 * mods_ref[mod0 + 2:mod0 + 3] * r


def _ffn(h, mods, norm_g, w_in, w_out, cond, *, half):
    tokens = h.shape[0]
    tm, tf = 512, 512
    nf = D_FF // tf
    layer = cond.layer
    return pl.pallas_call(
        functools.partial(_ffn_kernel, mod0=6 * half, g0=4 * half),
        grid=(tokens // tm, nf),
        in_specs=[
            pl.BlockSpec((tm, D_MODEL), lambda i, j: (i, 0)),
            *cond.specs(tm),
            pl.BlockSpec((None, None, D_MODEL, tf), lambda i, j: (layer, half, 0, j)),
            pl.BlockSpec((None, None, D_MODEL, tf), lambda i, j: (layer, half, 0, nf + j)),
            pl.BlockSpec((None, None, tf, D_MODEL), lambda i, j: (layer, half, j, 0)),
        ],
        out_specs=pl.BlockSpec((tm, D_MODEL), lambda i, j: (i, 0)),
        out_shape=jax.ShapeDtypeStruct((tokens, D_MODEL), F32),
        scratch_shapes=[pltpu.VMEM((tm, D_MODEL), BF16), pltpu.VMEM((tm, D_MODEL), F32)],
        compiler_params=_params(("parallel", "arbitrary")),
        name="ffn",
    )(h, mods, norm_g, w_in, w_in, w_out)


def _shift_rows(x, d, pos, seq):
    n = x.shape[0]
    y = pltpu.roll(x, (-d) % n, axis=0)
    valid = (pos + d >= 0) & (pos + d < seq)
    return jnp.where(valid, y, 0.0)


def _pool_delta(x, window, pos, seq):
    half = window // 2
    fwd = x
    bwd = x
    m = 1
    while m < half:
        fwd = fwd + _shift_rows(fwd, m, pos, seq)
        bwd = bwd + _shift_rows(bwd, -m, pos, seq)
        m *= 2
    total = fwd + _shift_rows(bwd, -1, pos, seq)
    count = jnp.minimum(pos + half, seq) - jnp.maximum(pos - half, 0)
    return total / count.astype(F32) - x


def _rope(x, cos, sin_lo, sin_hi):
    quarter = HEAD_DIM // 4
    return (x * cos + pltpu.roll(x, HEAD_DIM - quarter, axis=1) * sin_lo
            + pltpu.roll(x, quarter, axis=1) * sin_hi)


def _even_in_kernel(h_ref, mods_ref, g_ref, w_ref, pw_ref, ps_ref, qkg_ref, *rest, seq, rope):
    if rope:
        cos_ref, slo_ref, shi_ref, pool_ref, q_ref, k_ref, v_ref, u_ref = rest
    else:
        pool_ref, q_ref, k_ref, v_ref, u_ref = rest
    j = pl.program_id(1)
    tm = h_ref.shape[0]

    @pl.when(j == 0)
    def _():
        u = _rms(h_ref[...], g_ref[2:3]) * (1 + mods_ref[4:5]) + mods_ref[3:4]
        u_ref[...] = u.astype(BF16)

    def project():
        return jnp.dot(u_ref[...], w_ref[...], preferred_element_type=F32)

    def head(x, g):
        x = _rms(x, g)
        if rope:
            x = _rope(x, cos_ref[...], slo_ref[...], shi_ref[...])
        return x

    for jj in range(2):
        @pl.when(j == jj)
        def _(jj=jj):
            p = project()
            pos = lax.broadcasted_iota(jnp.int32, (tm, 1), 0) % seq
            for gi in range(2):
                sl = slice(gi * POOL_GROUP_DIM, (gi + 1) * POOL_GROUP_DIM)
                d = _pool_delta(p[:, sl], POOL_WINDOWS[2 * jj + gi], pos, seq)
                y = jnp.dot(d.astype(BF16), pw_ref[gi], preferred_element_type=F32)
                pool_ref[:, sl] = (y * ps_ref[:, sl]).astype(BF16)

    @pl.when((j == 2) | (j == 3))
    def _():
        p = project()
        for hd in range(Q_PER_KV):
            sl = slice(hd * HEAD_DIM, (hd + 1) * HEAD_DIM)
            q_ref[:, sl] = head(p[:, sl], qkg_ref[0:1]).astype(BF16)

    @pl.when(j == 4)
    def _():
        p = project()
        for hd in range(N_KV_HEADS):
            sl = slice(hd * HEAD_DIM, (hd + 1) * HEAD_DIM)
            k_ref[:, sl] = head(p[:, sl], qkg_ref[1:2])
        v_ref[...] = p[:, KV_WIDTH:]


def _rope_tables(seq):
    pos = jnp.arange(seq)
    row = (pos // GRID_W).astype(F32)
    col = (pos % GRID_W).astype(F32)
    quarter = HEAD_DIM // 4
    inv_freq = ROPE_THETA ** (-jnp.arange(quarter, dtype=F32) / quarter)
    ang_r = row[:, None] * inv_freq[None]
    ang_c = col[:, None] * inv_freq[None]
    zero = jnp.zeros_like(ang_r)
    cos = jnp.concatenate([jnp.cos(ang_r), jnp.cos(ang_r), jnp.cos(ang_c), jnp.cos(ang_c)], axis=-1)
    sin_lo = jnp.concatenate([-jnp.sin(ang_r), zero, -jnp.sin(ang_c), zero], axis=-1)
    sin_hi = jnp.concatenate([zero, jnp.sin(ang_r), zero, jnp.sin(ang_c)], axis=-1)
    return cos, sin_lo, sin_hi


def _even_in(h, mods, norm_g, w_in, pool_w, pool_scale, qk_g, cond, *, e, rope):
    tokens = h.shape[0]
    seq = cond.seq
    tm, tn = 1024, 512
    assert tm % seq == 0 and tokens % tm == 0
    n_pool = POOL_WIDTH // tn
    n_q = ATTN_WIDTH // tn
    in_specs = [
        pl.BlockSpec((tm, D_MODEL), lambda i, j: (i, 0)),
        *cond.specs(tm),
        pl.BlockSpec((None, D_MODEL, tn), lambda i, j: (e, 0, j)),
        pl.BlockSpec((None, 2, POOL_GROUP_DIM, POOL_GROUP_DIM),
                     lambda i, j: (e, jnp.minimum(j, n_pool - 1), 0, 0)),
        pl.BlockSpec((None, 1, tn), lambda i, j: (e, 0, jnp.minimum(j, n_pool - 1))),
        pl.BlockSpec((None, 2, HEAD_DIM), lambda i, j: (e, 0, 0)),
    ]
    args = [h, mods, norm_g, w_in, pool_w, pool_scale, qk_g]
    if rope:
        assert tm == seq
        in_specs += [pl.BlockSpec((tm, HEAD_DIM), lambda i, j: (0, 0))] * 3
        args += list(_rope_tables(seq))
    return pl.pallas_call(
        functools.partial(_even_in_kernel, seq=seq, rope=rope),
        grid=(tokens // tm, MIX_IN // tn),
        in_specs=in_specs,
        out_specs=[
            pl.BlockSpec((tm, tn), lambda i, j: (i, jnp.minimum(j, n_pool - 1))),
            pl.BlockSpec((tm, tn), lambda i, j: (i, jnp.clip(j - n_pool, 0, n_q - 1))),
            pl.BlockSpec((tm, KV_WIDTH), lambda i, j: (i, 0)),
            pl.BlockSpec((tm, KV_WIDTH), lambda i, j: (i, 0)),
        ],
        out_shape=[
            jax.ShapeDtypeStruct((tokens, POOL_WIDTH), BF16),
            jax.ShapeDtypeStruct((tokens, ATTN_WIDTH), BF16),
            jax.ShapeDtypeStruct((tokens, KV_WIDTH), F32),
            jax.ShapeDtypeStruct((tokens, KV_WIDTH), F32),
        ],
        scratch_shapes=[pltpu.VMEM((tm, D_MODEL), BF16)],
        compiler_params=_params(("parallel", "arbitrary")),
        name="even_in",
    )(*args)


def _attn_kernel(q_ref, k_ref, v_ref, *rest, cached):
    if cached:
        ck_ref, cv_ref, o_ref = rest
    else:
        (o_ref,) = rest
    scale = HEAD_DIM ** -0.5
    nt = (((1,), (1,)), ((), ()))
    kb = k_ref[...].astype(BF16)
    vb = v_ref[...].astype(BF16)
    if cached:
        ckb = ck_ref[...].astype(BF16)
        cvb = cv_ref[...].astype(BF16)
    for hd in range(Q_PER_KV):
        sl = slice(hd * HEAD_DIM, (hd + 1) * HEAD_DIM)
        q = q_ref[:, sl]
        s = lax.dot_general(q, kb, nt, preferred_element_type=F32) * scale
        m = jnp.max(s, axis=-1, keepdims=True)
        if cached:
            sc = lax.dot_general(q, ckb, nt, preferred_element_type=F32) * scale
            m = jnp.maximum(m, jnp.max(sc, axis=-1, keepdims=True))
        e = jnp.exp(s - m)
        denom = jnp.sum(e, axis=-1, keepdims=True)
        o = jnp.dot(e.astype(BF16), vb, preferred_element_type=F32)
        if cached:
            ec = jnp.exp(sc - m)
            denom = denom + jnp.sum(ec, axis=-1, keepdims=True)
            o = o + jnp.dot(ec.astype(BF16), cvb, preferred_element_type=F32)
        o_ref[:, sl] = (o / denom).astype(BF16)


def _attention(q, k, v, cache_k, cache_v, *, e, batch, seq):
    tokens = q.shape[0]
    tq = 256
    nq = seq // tq
    gw = Q_PER_KV * HEAD_DIM
    cached = cache_k is not None
    in_specs = [
        pl.BlockSpec((tq, gw), lambda b, g, i: (b * nq + i, g)),
        pl.BlockSpec((seq, HEAD_DIM), lambda b, g, i: (b, g)),
        pl.BlockSpec((seq, HEAD_DIM), lambda b, g, i: (b, g)),
    ]
    args = [q, k, v]
    if cached:
        past = cache_k.shape[2]
        in_specs += [pl.BlockSpec((None, None, past, HEAD_DIM), lambda b, g, i: (b, e, 0, g))] * 2
        args += [cache_k, cache_v]
    return pl.pallas_call(
        functools.partial(_attn_kernel, cached=cached),
        grid=(batch, N_KV_HEADS, nq),
        in_specs=in_specs,
        out_specs=pl.BlockSpec((tq, gw), lambda b, g, i: (b * nq + i, g)),
        out_shape=jax.ShapeDtypeStruct((tokens, ATTN_WIDTH), BF16),
        compiler_params=_params(("parallel", "parallel", "parallel")),
        name="attention",
    )(*args)


def _even_out_kernel(pool_ref, attn_ref, w_ref, h_ref, mods_ref, g_ref, out_ref):
    y = jnp.dot(pool_ref[...], w_ref[:POOL_WIDTH], preferred_element_type=F32)
    y = y + jnp.dot(attn_ref[...], w_ref[POOL_WIDTH:], preferred_element_type=F32)
    out_ref[...] = h_ref[...] + mods_ref[5:6] * _rms(y, g_ref[3:4])


def _even_out(pool_y, attn, w_out, h, mods, norm_g, cond, *, e):
    tokens = h.shape[0]
    tm = 512
    return pl.pallas_call(
        _even_out_kernel,
        grid=(tokens // tm,),
        in_specs=[
            pl.BlockSpec((tm, POOL_WIDTH), lambda i: (i, 0)),
            pl.BlockSpec((tm, ATTN_WIDTH), lambda i: (i, 0)),
            pl.BlockSpec((None, POOL_WIDTH + ATTN_WIDTH, D_MODEL), lambda i: (e, 0, 0)),
            pl.BlockSpec((tm, D_MODEL), lambda i: (i, 0)),
            *cond.specs(tm),
        ],
        out_specs=pl.BlockSpec((tm, D_MODEL), lambda i: (i, 0)),
        out_shape=jax.ShapeDtypeStruct((tokens, D_MODEL), F32),
        compiler_params=_params(("parallel",)),
        name="even_out",
    )(pool_y, attn, w_out, h, mods, norm_g)


def _odd_out_kernel(x_ref, w_ref, h_ref, mods_ref, g_ref, out_ref):
    k = pl.program_id(1)
    y = jnp.dot(x_ref[0], w_ref[:GROUP_WIDTH], preferred_element_type=F32)
    for g in range(1, x_ref.shape[0]):
        y = y + jnp.dot(x_ref[g], w_ref[g * GROUP_WIDTH:(g + 1) * GROUP_WIDTH], preferred_element_type=F32)

    @pl.when(k == 0)
    def _():
        out_ref[...] = y

    @pl.when(k == pl.num_programs(1) - 1)
    def _():
        out_ref[...] = h_ref[...] + mods_ref[5:6] * _rms(out_ref[...] + y, g_ref[3:4])


def _odd_out(yn, w_out, h, mods, norm_g, cond, *, o):
    tokens = h.shape[0]
    tm, nk = 512, 2
    gk = SSM_GROUPS // nk
    return pl.pallas_call(
        _odd_out_kernel,
        grid=(tokens // tm, nk),
        in_specs=[
            pl.BlockSpec((gk, tm, GROUP_WIDTH), lambda i, k: (k, i, 0)),
            pl.BlockSpec((None, gk * GROUP_WIDTH, D_MODEL), lambda i, k: (o, k, 0)),
            pl.BlockSpec((tm, D_MODEL), lambda i, k: (i, 0)),
            *cond.specs(tm),
        ],
        out_specs=pl.BlockSpec((tm, D_MODEL), lambda i, k: (i, 0)),
        out_shape=jax.ShapeDtypeStruct((tokens, D_MODEL), F32),
        compiler_params=_params(("parallel", "arbitrary")),
        name="odd_out",
    )(yn, w_out, h, mods, norm_g)


N_Z = D_INNER // GROUP_WIDTH
N_XBC = CONV_DIM // GROUP_WIDTH
X_CHUNKS = D_INNER // GROUP_WIDTH
BC_CHUNKS = SSM_GROUPS * D_STATE // GROUP_WIDTH
GROUPS_PER_CHUNK = GROUP_WIDTH // D_STATE


def _odd_in_kernel(h_ref, mods_ref, g_ref, w_ref, wdt_ref, cw_ref, cb_ref, dtb_ref,
                   z_ref, xbc_ref, dt_ref, u_ref, *, seq):
    j = pl.program_id(1)
    tm = h_ref.shape[0]

    @pl.when(j == 0)
    def _():
        u = _rms(h_ref[...], g_ref[2:3]) * (1 + mods_ref[4:5]) + mods_ref[3:4]
        u_ref[...] = u.astype(BF16)
        raw = jnp.dot(u_ref[...], wdt_ref[...], preferred_element_type=F32) + dtb_ref[...]
        dt_ref[...] = jnp.maximum(raw, 0.0) + jnp.log1p(jnp.exp(-jnp.abs(raw)))

    @pl.when(j < N_Z)
    def _():
        z_ref[...] = jnp.dot(u_ref[...], w_ref[...], preferred_element_type=F32)

    @pl.when(j >= N_Z)
    def _():
        p = jnp.dot(u_ref[...], w_ref[...], preferred_element_type=F32)
        pos = lax.broadcasted_iota(jnp.int32, (tm, 1), 0) % seq
        y = (_shift_rows(p, -1, pos, seq) * cw_ref[0:1] + p * cw_ref[1:2]
             + _shift_rows(p, 1, pos, seq) * cw_ref[2:3] + cb_ref[...])
        xbc_ref[...] = _silu(y)


def _odd_in(h, mods, norm_g, w_in, conv_w, conv_b, dt_bias, cond, *, o):
    tokens = h.shape[0]
    seq = cond.seq
    tm, tn = 1024, GROUP_WIDTH
    assert tm % seq == 0 and tokens % tm == 0
    dt_block = (D_INNER + CONV_DIM) // LANES
    conv_idx = lambda j: jnp.maximum(j - N_Z, 0)
    return pl.pallas_call(
        functools.partial(_odd_in_kernel, seq=seq),
        grid=(tokens // tm, N_Z + N_XBC),
        in_specs=[
            pl.BlockSpec((tm, D_MODEL), lambda i, j: (i, 0)),
            *cond.specs(tm),
            pl.BlockSpec((None, D_MODEL, tn), lambda i, j: (o, 0, j)),
            pl.BlockSpec((None, D_MODEL, 2 * SSM_HEADS), lambda i, j: (o, 0, dt_block)),
            pl.BlockSpec((None, 3, tn), lambda i, j: (o, 0, conv_idx(j))),
            pl.BlockSpec((None, 1, tn), lambda i, j: (o, 0, conv_idx(j))),
            pl.BlockSpec((None, 1, 2 * SSM_HEADS), lambda i, j: (o, 0, 0)),
        ],
        out_specs=[
            pl.BlockSpec((None, tm, tn), lambda i, j: (jnp.minimum(j, N_Z - 1), i, 0)),
            pl.BlockSpec((None, tm, tn), lambda i, j: (conv_idx(j), i, 0)),
            pl.BlockSpec((tm, 2 * SSM_HEADS), lambda i, j: (i, 0)),
        ],
        out_shape=[
            jax.ShapeDtypeStruct((N_Z, tokens, tn), F32),
            jax.ShapeDtypeStruct((N_XBC, tokens, tn), F32),
            jax.ShapeDtypeStruct((tokens, 2 * SSM_HEADS), F32),
        ],
        scratch_shapes=[pltpu.VMEM((tm, D_MODEL), BF16)],
        compiler_params=_params(("parallel", "arbitrary")),
        name="odd_in",
    )(h, mods, norm_g, w_in, w_in, conv_w, conv_b, dt_bias)


def _split3(x):
    x1 = x.astype(BF16)
    r1 = x - x1.astype(F32)
    x2 = r1.astype(BF16)
    x3 = (r1 - x2.astype(F32)).astype(BF16)
    return x1, x2, x3


def _ssd_kernel(*refs, direction, has_h0, emit_state, combine):
    refs = list(refs)
    x_ref, b_ref, c_ref, dt_ref, alog_ref = refs[:5]
    del refs[:5]
    h0_ref = refs.pop(0) if has_h0 else None
    if combine:
        yf_ref, z_ref, dskip_ref, ng_ref = refs[:4]
        del refs[:4]
        if emit_state:
            del refs[0]
    y_ref = refs.pop(0)
    hT_out_ref = refs.pop(0) if emit_state else None
    ht_ref = refs.pop(0)
    yb_ref = refs.pop(0) if combine else None

    c = pl.program_id(1)
    q = SSM_CHUNK
    pairs = GROUP_WIDTH // LANES

    @pl.when(c == 0)
    def _():
        if has_h0:
            for g in range(SSM_GROUPS):
                for pr in range(pairs):
                    r0 = g * GROUP_WIDTH + pr * LANES
                    ht_ref[g, :, pr * LANES:(pr + 1) * LANES] = h0_ref[r0:r0 + LANES, :].T
        else:
            ht_ref[...] = jnp.zeros_like(ht_ref)

    ii = lax.broadcasted_iota(jnp.int32, (q, q), 0)
    jj = lax.broadcasted_iota(jnp.int32, (q, q), 1)
    causal = (jj <= ii) if direction == 0 else (jj >= ii)
    tri = jnp.where(causal, 1.0, 0.0).astype(BF16)
    low_lane = lax.broadcasted_iota(jnp.int32, (1, LANES), 1) < SSM_HEADDIM

    dt = dt_ref[...]
    dta = dt * (-jnp.exp(alog_ref[...]))
    a = sum(jnp.dot(tri, part, preferred_element_type=F32) for part in _split3(dta))
    a_end = a[q - 1:q] if direction == 0 else a[0:1]
    a_t = a.T
    dt_t = dt.T
    s_t = (dt * jnp.exp(a_end - a)).T
    end_scale = jnp.exp(a_end)

    for g in range(SSM_GROUPS):
        bc_lanes = slice((g % GROUPS_PER_CHUNK) * D_STATE, (g % GROUPS_PER_CHUNK + 1) * D_STATE)
        bg_t = b_ref[g // GROUPS_PER_CHUNK, :, bc_lanes].T
        cg32 = c_ref[g // GROUPS_PER_CHUNK, :, bc_lanes]
        cb = jnp.dot(cg32.astype(BF16), bg_t.astype(BF16), preferred_element_type=F32)
        for pr in range(pairs):
            lanes = slice(pr * LANES, (pr + 1) * LANES)
            xp = x_ref[g, :, lanes].astype(BF16)
            hp = ht_ref[g, :, lanes]
            hpb = hp.astype(BF16)
            ys, ds, scales = [], [], []
            for e in range(2):
                hl = direction * SSM_HEADS + g * HEADS_PER_GROUP + 2 * pr + e
                col = jnp.broadcast_to(a[:, hl:hl + 1], (q, q))
                decay = jnp.exp(jnp.where(causal, col - a_t[hl:hl + 1, :], -jnp.inf))
                w = (cb * decay * dt_t[hl:hl + 1, :]).astype(BF16)
                cdec = (cg32 * jnp.exp(col)).astype(BF16)
                ys.append(jnp.dot(w, xp, preferred_element_type=F32)
                          + jnp.dot(cdec, hpb, preferred_element_type=F32))
                bs = (bg_t * s_t[hl:hl + 1, :]).astype(BF16)
                ds.append(jnp.dot(bs, xp, preferred_element_type=F32))
                scales.append(jnp.broadcast_to(end_scale[:, hl:hl + 1], (1, LANES)))
            y_pair = jnp.where(low_lane, ys[0], ys[1])
            if combine:
                yb_ref[g, :, lanes] = y_pair
            else:
                y_ref[g, :, lanes] = y_pair
            ht_ref[g, :, lanes] = (hp * jnp.where(low_lane, scales[0], scales[1])
                                   + jnp.where(low_lane, ds[0], ds[1]))

    if combine:
        ssq = jnp.zeros((q, 1), F32)
        for g in range(SSM_GROUPS):
            xg = x_ref[g]
            y = (yf_ref[g] + dskip_ref[0, g] * xg) + (yb_ref[g] + dskip_ref[1, g] * xg)
            y = y * _silu(z_ref[g])
            yb_ref[g] = y
            ssq = ssq + jnp.sum(y * y, axis=-1, keepdims=True)
        inv = lax.rsqrt(ssq / D_INNER + EPS)
        for g in range(SSM_GROUPS):
            y_ref[g] = (yb_ref[g] * inv * ng_ref[g]).astype(BF16)

    if emit_state:
        @pl.when(c == pl.num_programs(1) - 1)
        def _():
            for g in range(SSM_GROUPS):
                for pr in range(pairs):
                    r0 = g * GROUP_WIDTH + pr * LANES
                    hT_out_ref[r0:r0 + LANES, :] = ht_ref[g, :, pr * LANES:(pr + 1) * LANES].T


def _ssd(xbc, dt, a_log, h0, *, o, direction, batch, seq, final_state=None, combine=None):
    tokens = xbc.shape[1]
    q = SSM_CHUNK
    nc = seq // q
    if direction == 0:
        chunk = lambda b, c: b * nc + c
    else:
        chunk = lambda b, c: b * nc + (nc - 1 - c)
    rows = lambda n, first: pl.BlockSpec((n, q, GROUP_WIDTH), lambda b, c: (first // n, chunk(b, c), 0))
    in_specs = [rows(X_CHUNKS, 0), rows(BC_CHUNKS, X_CHUNKS), rows(BC_CHUNKS, X_CHUNKS + BC_CHUNKS),
                pl.BlockSpec((q, 2 * SSM_HEADS), lambda b, c: (chunk(b, c), 0)),
                pl.BlockSpec((None, 1, 2 * SSM_HEADS), lambda b, c: (o, 0, 0))]
    args = [xbc, xbc, xbc, dt, a_log]
    if h0 is not None:
        in_specs.append(pl.BlockSpec((None, None, None, D_INNER, D_STATE), lambda b, c: (b, o, direction, 0, 0)))
        args.append(h0)
    if combine is not None:
        y_f, z, d_skip, norm_g = combine
        in_specs += [rows(X_CHUNKS, 0), rows(N_Z, 0),
                     pl.BlockSpec((None, 2, SSM_GROUPS, 1, GROUP_WIDTH), lambda b, c: (o, 0, 0, 0, 0)),
                     pl.BlockSpec((None, SSM_GROUPS, 1, GROUP_WIDTH), lambda b, c: (o, 0, 0, 0))]
        args += [y_f, z, d_skip, norm_g]
    out_specs = [rows(X_CHUNKS, 0)]
    out_shape = [jax.ShapeDtypeStruct((X_CHUNKS, tokens, GROUP_WIDTH), BF16 if combine is not None else F32)]
    aliases = {}
    emit_state = final_state is not None
    if emit_state:
        out_specs.append(pl.BlockSpec((None, None, D_INNER, D_STATE), lambda b, c: (b, direction, 0, 0)))
        out_shape.append(jax.ShapeDtypeStruct((batch, 2, D_INNER, D_STATE), F32))
        if not isinstance(final_state, str):
            assert combine is not None
            in_specs.append(pl.BlockSpec(memory_space=pl.ANY))
            aliases = {len(args): 1}
            args.append(final_state)
    scratch = [pltpu.VMEM((SSM_GROUPS, D_STATE, GROUP_WIDTH), F32)]
    if combine is not None:
        scratch.append(pltpu.VMEM((SSM_GROUPS, q, GROUP_WIDTH), F32))
    return pl.pallas_call(
        functools.partial(_ssd_kernel, direction=direction, has_h0=h0 is not None,
                          emit_state=emit_state, combine=combine is not None),
        grid=(batch, nc),
        in_specs=in_specs,
        out_specs=out_specs,
        out_shape=out_shape,
        scratch_shapes=scratch,
        input_output_aliases=aliases,
        compiler_params=_params(("parallel", "arbitrary")),
        name="ssd_fwd" if direction == 0 else "ssd_bwd",
    )(*args)


def _trunk(h, mods, w, *, row0, n_cond, batch, seq, cache_k=None, cache_v=None, state=None):
    context = cache_k is None
    outs = {}
    for l in range(DEPTH):
        cond = _Cond(l, row0, n_cond, seq)
        norm_g = w["norm_g"]
        h = _ffn(h, mods, norm_g, w["ffn_w_in"], w["ffn_w_out"], cond, half=0)
        if l % 2 == 0:
            e = l // 2
            pool_y, q, k, v = _even_in(h, mods, norm_g, w["mix_w_in"], w["pool_w"], w["pool_scale"],
                                       w["qk_norm_g"], cond, e=e, rope=not context)
            attn = _attention(q, k, v, cache_k, cache_v, e=e, batch=batch, seq=seq)
            h = _even_out(pool_y, attn, w["mix_w_out"], h, mods, norm_g, cond, e=e)
            outs["k"], outs["v"] = k, v
        else:
            o = l // 2
            z, xbc, dt = _odd_in(h, mods, norm_g, w["ssm_w_in"], w["ssm_conv_w"], w["ssm_conv_b"],
                                 w["ssm_dt_bias"], cond, o=o)
            scan = functools.partial(_ssd, xbc, dt, w["ssm_A_log"], state, o=o, batch=batch, seq=seq)
            fwd = scan(direction=0, final_state="new" if context else None)
            bwd = scan(direction=1, final_state=fwd[1] if context else None,
                       combine=(fwd[0], z, w["ssm_D"], w["ssm_norm_g"]))
            if context:
                outs["ssm"] = bwd[1]
            h = _odd_out(bwd[0], w["ssm_w_out"], h, mods, norm_g, cond, o=o)
        h = _ffn(h, mods, norm_g, w["ffn_w_in"], w["ffn_w_out"], cond, half=1)
    return h, outs


def kernel(x_prompt, x_sample, cache_k, cache_v, state_ssm, c, c_ctx, ada_w, ada_b, norm_g, ffn_w_in,
           ffn_w_out, mix_w_in, pool_w, pool_scale, qk_norm_g, mix_w_out, ssm_w_in, ssm_conv_w,
           ssm_conv_b, ssm_dt_bias, ssm_A_log, ssm_D, ssm_norm_g, ssm_w_out):
    batch, seq, _ = x_prompt.shape
    dec_batch, dec_seq, _ = x_sample.shape
    n_even, n_odd = mix_w_in.shape[0], ssm_w_in.shape[0]
    past = cache_k.shape[2]

    cond = jnp.zeros((COND_ROWS, D_MODEL), F32).at[0].set(c_ctx).at[1:1 + dec_batch].set(c)
    mods = _ada_mods(cond, ada_w, ada_b).reshape(DEPTH, COND_ROWS, N_MOD, D_MODEL)

    w = dict(
        norm_g=norm_g,
        ffn_w_in=ffn_w_in.astype(BF16), ffn_w_out=ffn_w_out.astype(BF16),
        mix_w_in=mix_w_in.astype(BF16), pool_w=pool_w.astype(BF16),
        pool_scale=pool_scale.reshape(n_even, 1, POOL_WIDTH), qk_norm_g=qk_norm_g,
        mix_w_out=mix_w_out.astype(BF16),
        ssm_w_in=ssm_w_in.astype(BF16),
        ssm_conv_w=jnp.swapaxes(ssm_conv_w, 1, 2), ssm_conv_b=ssm_conv_b.reshape(n_odd, 1, CONV_DIM),
        ssm_dt_bias=ssm_dt_bias.reshape(n_odd, 1, 2 * SSM_HEADS),
        ssm_A_log=ssm_A_log.reshape(n_odd, 1, 2 * SSM_HEADS),
        ssm_D=jnp.repeat(ssm_D, SSM_HEADDIM, axis=-1).reshape(n_odd, 2, SSM_GROUPS, 1, GROUP_WIDTH),
        ssm_norm_g=ssm_norm_g.reshape(n_odd, SSM_GROUPS, 1, GROUP_WIDTH),
        ssm_w_out=ssm_w_out.astype(BF16),
    )

    y_prompt, ctx = _trunk(x_prompt.reshape(batch * seq, D_MODEL), mods, w, row0=0, n_cond=1,
                           batch=batch, seq=seq)
    y_sample, _ = _trunk(x_sample.reshape(dec_batch * dec_seq, D_MODEL), mods, w, row0=1, n_cond=dec_batch,
                         batch=dec_batch, seq=dec_seq,
                         cache_k=cache_k.reshape(dec_batch, n_even, past, KV_WIDTH),
                         cache_v=cache_v.reshape(dec_batch, n_even, past, KV_WIDTH),
                         state=state_ssm.reshape(dec_batch, n_odd, 2, D_INNER, D_STATE))

    new_k = ctx["k"].reshape(batch, n_even, seq, N_KV_HEADS, HEAD_DIM)
    new_v = ctx["v"].reshape(batch, n_even, seq, N_KV_HEADS, HEAD_DIM)
    new_ssm = ctx["ssm"].reshape(batch, n_odd, 2, SSM_HEADS, SSM_HEADDIM, D_STATE)
    return (y_prompt.reshape(batch, seq, D_MODEL), y_sample.reshape(dec_batch, dec_seq, D_MODEL),
            new_k, new_v, new_ssm)
```

```python
import functools

import jax
import jax.numpy as jnp
from jax import lax
from jax.experimental import pallas as pl
from jax.experimental.pallas import tpu as pltpu

F32 = jnp.float32
BF16 = jnp.bfloat16

D_MODEL = 2048
DEPTH = 2
GRID_W = 64
EPS = 1e-6
N_MOD = 9
N_NORM = 6
D_FF = 5632
POOL_WINDOWS = (2, 4, 8, 16)
POOL_WIDTH = 1024
POOL_GROUP_DIM = 256
HEAD_DIM = 128
N_KV_HEADS = 2
Q_PER_KV = 4
ATTN_WIDTH = 1024
KV_WIDTH = 256
MIX_IN = 2560
ROPE_THETA = 10000.0
D_INNER = 4096
SSM_HEADDIM = 64
SSM_HEADS = 64
SSM_GROUPS = 8
HEADS_PER_GROUP = 8
D_STATE = 128
SSM_CHUNK = 128
CONV_DIM = 6144
GROUP_WIDTH = HEADS_PER_GROUP * SSM_HEADDIM

LANES = 128
VMEM_LIMIT = 56 * 1024 * 1024
COND_ROWS = 16


def _params(sem):
    return pltpu.CompilerParams(dimension_semantics=sem, vmem_limit_bytes=VMEM_LIMIT)


def _rms(x, g):
    ms = jnp.mean(x * x, axis=-1, keepdims=True)
    return x * lax.rsqrt(ms + EPS) * g


def _silu(x):
    return x * jax.nn.sigmoid(x)


ROW_BLOCK = 16


def _modulate_rows(h_ref, g, shift, scale, u_ref):
    gain = g * (1 + scale)
    for r in range(0, h_ref.shape[0], ROW_BLOCK):
        x = h_ref[r:r + ROW_BLOCK]
        ms = jnp.mean(x * x, axis=-1, keepdims=True)
        u_ref[r:r + ROW_BLOCK] = (x * lax.rsqrt(ms + EPS) * gain + shift).astype(BF16)


def _residual_rows(h_ref, y_ref, g, gate, out_ref):
    gain = g * gate
    for r in range(0, h_ref.shape[0], ROW_BLOCK):
        y = y_ref[r:r + ROW_BLOCK]
        ms = jnp.mean(y * y, axis=-1, keepdims=True)
        out_ref[r:r + ROW_BLOCK] = h_ref[r:r + ROW_BLOCK] + y * lax.rsqrt(ms + EPS) * gain


class _Cond:
    def __init__(self, layer, row0, n_cond, seq):
        self.layer, self.row0, self.n_cond, self.seq = layer, row0, n_cond, seq

    def specs(self, tm):
        layer, row0 = self.layer, self.row0
        if self.n_cond == 1:
            row = lambda i: row0
        else:
            assert self.seq % tm == 0
            per_seq = self.seq // tm
            row = lambda i: row0 + i // per_seq
        return [pl.BlockSpec((None, None, N_MOD, D_MODEL), lambda i, *_: (layer, row(i), 0, 0)),
                pl.BlockSpec((None, N_NORM, D_MODEL), lambda *_: (layer, 0, 0))]


def _mods_kernel(cond_ref, w_ref, b_ref, out_ref):
    sc = _silu(cond_ref[...]).astype(BF16)
    w = w_ref[...].astype(BF16)
    out_ref[...] = jnp.dot(sc, w, preferred_element_type=F32) + b_ref[...]


def _ada_mods(cond, ada_w, ada_b):
    tn = 1024
    n_out = N_MOD * D_MODEL
    return pl.pallas_call(
        _mods_kernel,
        grid=(DEPTH, n_out // tn),
        in_specs=[
            pl.BlockSpec((COND_ROWS, D_MODEL), lambda l, n: (0, 0)),
            pl.BlockSpec((None, D_MODEL, tn), lambda l, n: (l, 0, n)),
            pl.BlockSpec((None, 1, tn), lambda l, n: (l, 0, n)),
        ],
        out_specs=pl.BlockSpec((None, COND_ROWS, tn), lambda l, n: (l, 0, n)),
        out_shape=jax.ShapeDtypeStruct((DEPTH, COND_ROWS, n_out), F32),
        compiler_params=_params(("parallel", "parallel")),
        name="ada_mods",
    )(cond, ada_w, ada_b.reshape(DEPTH, 1, n_out))


FFN_SUB = 256


def _ffn_kernel(h_ref, mods_ref, g_ref, wab_ref, wo_ref, out_ref, u_ref, acc_ref, ab_ref, hid_ref,
                *, mod0, g0):
    j = pl.program_id(1)
    n_sub = wo_ref.shape[0] // FFN_SUB

    def gate_rows(s):
        for r in range(0, ab_ref.shape[1], ROW_BLOCK):
            ab = ab_ref[s, r:r + ROW_BLOCK]
            hid_ref[s, r:r + ROW_BLOCK] = (_silu(ab[:, :FFN_SUB]) * ab[:, FFN_SUB:]).astype(BF16)

    def partial_out():
        u = u_ref[...]
        for s in range(n_sub):
            ab_ref[s] = jnp.dot(u, wab_ref[:, 2 * s * FFN_SUB:2 * (s + 1) * FFN_SUB], preferred_element_type=F32)
            gate_rows(s)
        y = jnp.dot(hid_ref[0], wo_ref[:FFN_SUB], preferred_element_type=F32)
        for s in range(1, n_sub):
            y = y + jnp.dot(hid_ref[s], wo_ref[s * FFN_SUB:(s + 1) * FFN_SUB], preferred_element_type=F32)
        return y

    @pl.when(j == 0)
    def _():
        _modulate_rows(h_ref, g_ref[g0:g0 + 1], mods_ref[mod0:mod0 + 1], mods_ref[mod0 + 1:mod0 + 2], u_ref)
        acc_ref[...] = partial_out()

    @pl.when(j > 0)
    def _():
        acc_ref[...] += partial_out()

    @pl.when(j == pl.num_programs(1) - 1)
    def _():
        _residual_rows(h_ref, acc_ref, g_ref[g0 + 1:g0 + 2], 0.5 * mods_ref[mod0 + 2:mod0 + 3], out_ref)


def _interleave_gate_columns(w_in):
    lead = w_in.shape[:-1]
    w = w_in.reshape(*lead, 2, D_FF // FFN_SUB, FFN_SUB)
    return jnp.swapaxes(w, -3, -2).reshape(*lead, 2 * D_FF)


def _ffn(h, mods, norm_g, w_in, w_out, cond, *, half):
    tokens = h.shape[0]
    tm, tf = 512, 512
    nf = D_FF // tf
    layer = cond.layer
    return pl.pallas_call(
        functools.partial(_ffn_kernel, mod0=6 * half, g0=4 * half),
        grid=(tokens // tm, nf),
        in_specs=[
            pl.BlockSpec((tm, D_MODEL), lambda i, j: (i, 0)),
            *cond.specs(tm),
            pl.BlockSpec((None, None, D_MODEL, 2 * tf), lambda i, j: (layer, half, 0, j)),
            pl.BlockSpec((None, None, tf, D_MODEL), lambda i, j: (layer, half, j, 0)),
        ],
        out_specs=pl.BlockSpec((tm, D_MODEL), lambda i, j: (i, 0)),
        out_shape=jax.ShapeDtypeStruct((tokens, D_MODEL), F32),
        scratch_shapes=[pltpu.VMEM((tm, D_MODEL), BF16), pltpu.VMEM((tm, D_MODEL), F32),
                        pltpu.VMEM((tf // FFN_SUB, tm, 2 * FFN_SUB), F32),
                        pltpu.VMEM((tf // FFN_SUB, tm, FFN_SUB), BF16)],
        compiler_params=_params(("parallel", "arbitrary")),
        name="ffn",
    )(h, mods, norm_g, w_in, w_out)


SUBLANES = 8


POOL_ROWS = 64


def _halo_block(p_ref, r, n_rows, cols, seq):
    zeros = jnp.zeros((SUBLANES, cols.stop - cols.start), F32)
    before = zeros if r % seq == 0 else p_ref[r - SUBLANES:r, cols]
    after = zeros if (r + n_rows) % seq == 0 else p_ref[r + n_rows:r + n_rows + SUBLANES, cols]
    return jnp.concatenate([before, p_ref[r:r + n_rows, cols], after], axis=0)


def _pool_delta_rows(p_ref, cols, window, seq, d_ref):
    half = window // 2
    assert half <= SUBLANES and seq % POOL_ROWS == 0
    n = p_ref.shape[0]
    rows = POOL_ROWS + 2 * SUBLANES
    row = lax.broadcasted_iota(jnp.int32, (POOL_ROWS, 1), 0)
    for r in range(0, n, POOL_ROWS):
        blk = _halo_block(p_ref, r, POOL_ROWS, cols, seq)
        fwd = blk
        bwd = blk
        m = 1
        while m < half:
            fwd = fwd + pltpu.roll(fwd, rows - m, axis=0)
            bwd = bwd + pltpu.roll(bwd, m, axis=0)
            m *= 2
        total = (fwd + pltpu.roll(bwd, 1, axis=0))[SUBLANES:SUBLANES + POOL_ROWS]
        pos = row + r % seq
        count = jnp.minimum(pos + half, seq) - jnp.maximum(pos - half, 0)
        d = total / count.astype(F32) - blk[SUBLANES:SUBLANES + POOL_ROWS]
        d_ref[r:r + POOL_ROWS, cols] = d.astype(BF16)


def _rope(x, cos, sin_lo, sin_hi):
    quarter = HEAD_DIM // 4
    return (x * cos + pltpu.roll(x, HEAD_DIM - quarter, axis=1) * sin_lo
            + pltpu.roll(x, quarter, axis=1) * sin_hi)


HEAD_ROWS_ROPE = 256


def _even_in_kernel(h_ref, mods_ref, g_ref, w_ref, pw_ref, ps_ref, qkg_ref, *rest, seq, rope):
    if rope:
        cos_ref, slo_ref, shi_ref, pool_ref, q_ref, k_ref, v_ref, u_ref, p_ref, d_ref = rest
    else:
        pool_ref, q_ref, k_ref, v_ref, u_ref, p_ref, d_ref = rest
    j = pl.program_id(1)
    tm = h_ref.shape[0]
    n_pool = POOL_WIDTH // p_ref.shape[2]
    n_q = ATTN_WIDTH // p_ref.shape[2]

    def project(chunk):
        p_ref[chunk % 2] = jnp.dot(u_ref[...], w_ref[...], preferred_element_type=F32)

    groups = p_ref.shape[2] // POOL_GROUP_DIM

    def pool_deltas(chunk):
        for gi in range(groups):
            cols = slice(gi * POOL_GROUP_DIM, (gi + 1) * POOL_GROUP_DIM)
            _pool_delta_rows(p_ref.at[chunk % 2], cols, POOL_WINDOWS[groups * chunk + gi], seq, d_ref)

    def pool_mix():
        for gi in range(groups):
            cols = slice(gi * POOL_GROUP_DIM, (gi + 1) * POOL_GROUP_DIM)
            y = jnp.dot(d_ref[:, cols], pw_ref[gi], preferred_element_type=F32)
            pool_ref[:, cols] = (y * ps_ref[:, cols]).astype(BF16)

    def heads_epilogue(chunk, n_heads, gain, dst_ref):
        src = p_ref.at[chunk % 2]
        step = HEAD_ROWS_ROPE if rope else ROW_BLOCK
        for r in range(0, tm, step):
            rows = slice(r, r + step)
            for hd in range(n_heads):
                cols = slice(hd * HEAD_DIM, (hd + 1) * HEAD_DIM)
                x = _rms(src[rows, cols], gain)
                if rope:
                    x = _rope(x, cos_ref[rows], slo_ref[rows], shi_ref[rows])
                dst_ref[rows, cols] = x.astype(dst_ref.dtype)

    def kv_epilogue(chunk):
        heads_epilogue(chunk, N_KV_HEADS, qkg_ref[1:2], k_ref)
        src = p_ref.at[chunk % 2]
        for r in range(0, tm, ROW_BLOCK):
            v_ref[r:r + ROW_BLOCK] = src[r:r + ROW_BLOCK, KV_WIDTH:]

    @pl.when(j == 0)
    def _():
        _modulate_rows(h_ref, g_ref[2:3], mods_ref[3:4], mods_ref[4:5], u_ref)
        project(0)

    for chunk in range(1, n_pool + n_q + 2):
        @pl.when(j == chunk)
        def _(chunk=chunk):
            done = chunk - 1
            if done < n_pool:
                pool_deltas(done)
            elif done < n_pool + n_q:
                heads_epilogue(done, Q_PER_KV, qkg_ref[0:1], q_ref)
            else:
                kv_epilogue(done)
            if chunk <= n_pool + n_q:
                project(chunk)
            if done < n_pool:
                pool_mix()


def _rope_tables(seq):
    pos = jnp.arange(seq)
    row = (pos // GRID_W).astype(F32)
    col = (pos % GRID_W).astype(F32)
    quarter = HEAD_DIM // 4
    inv_freq = ROPE_THETA ** (-jnp.arange(quarter, dtype=F32) / quarter)
    ang_r = row[:, None] * inv_freq[None]
    ang_c = col[:, None] * inv_freq[None]
    zero = jnp.zeros_like(ang_r)
    cos = jnp.concatenate([jnp.cos(ang_r), jnp.cos(ang_r), jnp.cos(ang_c), jnp.cos(ang_c)], axis=-1)
    sin_lo = jnp.concatenate([-jnp.sin(ang_r), zero, -jnp.sin(ang_c), zero], axis=-1)
    sin_hi = jnp.concatenate([zero, jnp.sin(ang_r), zero, jnp.sin(ang_c)], axis=-1)
    return cos, sin_lo, sin_hi


def _even_in(h, mods, norm_g, w_in, pool_w, pool_scale, qk_g, cond, *, e, rope):
    tokens = h.shape[0]
    seq = cond.seq
    tm, tn = 1024, 512
    assert tm % seq == 0 and tokens % tm == 0
    n_pool = POOL_WIDTH // tn
    n_q = ATTN_WIDTH // tn
    n_chunks = MIX_IN // tn
    pool_idx = lambda j: jnp.clip(j - 1, 0, n_pool - 1)
    q_idx = lambda j: jnp.clip(j - 1 - n_pool, 0, n_q - 1)
    in_specs = [
        pl.BlockSpec((tm, D_MODEL), lambda i, j: (i, 0)),
        *cond.specs(tm),
        pl.BlockSpec((None, D_MODEL, tn), lambda i, j: (e, 0, jnp.minimum(j, n_chunks - 1))),
        pl.BlockSpec((None, 2, POOL_GROUP_DIM, POOL_GROUP_DIM), lambda i, j: (e, pool_idx(j), 0, 0)),
        pl.BlockSpec((None, 1, tn), lambda i, j: (e, 0, pool_idx(j))),
        pl.BlockSpec((None, 2, HEAD_DIM), lambda i, j: (e, 0, 0)),
    ]
    args = [h, mods, norm_g, w_in, pool_w, pool_scale, qk_g]
    if rope:
        assert tm == seq
        in_specs += [pl.BlockSpec((tm, HEAD_DIM), lambda i, j: (0, 0))] * 3
        args += list(_rope_tables(seq))
    return pl.pallas_call(
        functools.partial(_even_in_kernel, seq=seq, rope=rope),
        grid=(tokens // tm, n_chunks + 1),
        in_specs=in_specs,
        out_specs=[
            pl.BlockSpec((tm, tn), lambda i, j: (i, pool_idx(j))),
            pl.BlockSpec((tm, tn), lambda i, j: (i, q_idx(j))),
            pl.BlockSpec((tm, KV_WIDTH), lambda i, j: (i, 0)),
            pl.BlockSpec((tm, KV_WIDTH), lambda i, j: (i, 0)),
        ],
        out_shape=[
            jax.ShapeDtypeStruct((tokens, POOL_WIDTH), BF16),
            jax.ShapeDtypeStruct((tokens, ATTN_WIDTH), BF16),
            jax.ShapeDtypeStruct((tokens, KV_WIDTH), F32),
            jax.ShapeDtypeStruct((tokens, KV_WIDTH), F32),
        ],
        scratch_shapes=[pltpu.VMEM((tm, D_MODEL), BF16), pltpu.VMEM((2, tm, tn), F32),
                        pltpu.VMEM((tm, tn), BF16)],
        compiler_params=_params(("parallel", "arbitrary")),
        name="even_in",
    )(*args)


def _attn_kernel(q_ref, k_ref, v_ref, *rest, cached):
    if cached:
        ck_ref, cv_ref, o_ref = rest
    else:
        (o_ref,) = rest
    scale = HEAD_DIM ** -0.5
    nt = (((1,), (1,)), ((), ()))
    kb = k_ref[...].astype(BF16)
    vb = v_ref[...].astype(BF16)
    if cached:
        ckb = ck_ref[...].astype(BF16)
        cvb = cv_ref[...].astype(BF16)
    for hd in range(Q_PER_KV):
        sl = slice(hd * HEAD_DIM, (hd + 1) * HEAD_DIM)
        q = q_ref[:, sl]
        s = lax.dot_general(q, kb, nt, preferred_element_type=F32) * scale
        m = jnp.max(s, axis=-1, keepdims=True)
        if cached:
            sc = lax.dot_general(q, ckb, nt, preferred_element_type=F32) * scale
            m = jnp.maximum(m, jnp.max(sc, axis=-1, keepdims=True))
        e = jnp.exp(s - m)
        denom = jnp.sum(e, axis=-1, keepdims=True)
        o = jnp.dot(e.astype(BF16), vb, preferred_element_type=F32)
        if cached:
            ec = jnp.exp(sc - m)
            denom = denom + jnp.sum(ec, axis=-1, keepdims=True)
            o = o + jnp.dot(ec.astype(BF16), cvb, preferred_element_type=F32)
        o_ref[:, sl] = (o / denom).astype(BF16)


def _attention(q, k, v, cache_k, cache_v, *, e, batch, seq):
    tokens = q.shape[0]
    tq = 256
    nq = seq // tq
    gw = Q_PER_KV * HEAD_DIM
    cached = cache_k is not None
    in_specs = [
        pl.BlockSpec((tq, gw), lambda b, g, i: (b * nq + i, g)),
        pl.BlockSpec((seq, HEAD_DIM), lambda b, g, i: (b, g)),
        pl.BlockSpec((seq, HEAD_DIM), lambda b, g, i: (b, g)),
    ]
    args = [q, k, v]
    if cached:
        past = cache_k.shape[2]
        in_specs += [pl.BlockSpec((None, None, past, HEAD_DIM), lambda b, g, i: (b, e, 0, g))] * 2
        args += [cache_k, cache_v]
    return pl.pallas_call(
        functools.partial(_attn_kernel, cached=cached),
        grid=(batch, N_KV_HEADS, nq),
        in_specs=in_specs,
        out_specs=pl.BlockSpec((tq, gw), lambda b, g, i: (b * nq + i, g)),
        out_shape=jax.ShapeDtypeStruct((tokens, ATTN_WIDTH), BF16),
        compiler_params=_params(("parallel", "parallel", "parallel")),
        name="attention",
    )(*args)


def _even_out_kernel(pool_ref, attn_ref, w_ref, h_ref, mods_ref, g_ref, out_ref):
    y = jnp.dot(pool_ref[...], w_ref[:POOL_WIDTH], preferred_element_type=F32)
    y = y + jnp.dot(attn_ref[...], w_ref[POOL_WIDTH:], preferred_element_type=F32)
    out_ref[...] = y
    _residual_rows(h_ref, out_ref, g_ref[3:4], mods_ref[5:6], out_ref)


def _even_out(pool_y, attn, w_out, h, mods, norm_g, cond, *, e):
    tokens = h.shape[0]
    tm = 512
    return pl.pallas_call(
        _even_out_kernel,
        grid=(tokens // tm,),
        in_specs=[
            pl.BlockSpec((tm, POOL_WIDTH), lambda i: (i, 0)),
            pl.BlockSpec((tm, ATTN_WIDTH), lambda i: (i, 0)),
            pl.BlockSpec((None, POOL_WIDTH + ATTN_WIDTH, D_MODEL), lambda i: (e, 0, 0)),
            pl.BlockSpec((tm, D_MODEL), lambda i: (i, 0)),
            *cond.specs(tm),
        ],
        out_specs=pl.BlockSpec((tm, D_MODEL), lambda i: (i, 0)),
        out_shape=jax.ShapeDtypeStruct((tokens, D_MODEL), F32),
        compiler_params=_params(("parallel",)),
        name="even_out",
    )(pool_y, attn, w_out, h, mods, norm_g)


def _odd_out_kernel(x_ref, w_ref, h_ref, mods_ref, g_ref, out_ref):
    k = pl.program_id(1)
    y = jnp.dot(x_ref[0], w_ref[:GROUP_WIDTH], preferred_element_type=F32)
    for g in range(1, x_ref.shape[0]):
        y = y + jnp.dot(x_ref[g], w_ref[g * GROUP_WIDTH:(g + 1) * GROUP_WIDTH], preferred_element_type=F32)

    @pl.when(k == 0)
    def _():
        out_ref[...] = y

    @pl.when(k == pl.num_programs(1) - 1)
    def _():
        out_ref[...] += y
        _residual_rows(h_ref, out_ref, g_ref[3:4], mods_ref[5:6], out_ref)


def _odd_out(yn, w_out, h, mods, norm_g, cond, *, o):
    tokens = h.shape[0]
    tm, nk = 512, 2
    gk = SSM_GROUPS // nk
    return pl.pallas_call(
        _odd_out_kernel,
        grid=(tokens // tm, nk),
        in_specs=[
            pl.BlockSpec((gk, tm, GROUP_WIDTH), lambda i, k: (k, i, 0)),
            pl.BlockSpec((None, gk * GROUP_WIDTH, D_MODEL), lambda i, k: (o, k, 0)),
            pl.BlockSpec((tm, D_MODEL), lambda i, k: (i, 0)),
            *cond.specs(tm),
        ],
        out_specs=pl.BlockSpec((tm, D_MODEL), lambda i, k: (i, 0)),
        out_shape=jax.ShapeDtypeStruct((tokens, D_MODEL), F32),
        compiler_params=_params(("parallel", "arbitrary")),
        name="odd_out",
    )(yn, w_out, h, mods, norm_g)


N_Z = D_INNER // GROUP_WIDTH
N_XBC = CONV_DIM // GROUP_WIDTH
X_CHUNKS = D_INNER // GROUP_WIDTH
BC_CHUNKS = SSM_GROUPS * D_STATE // GROUP_WIDTH
GROUPS_PER_CHUNK = GROUP_WIDTH // D_STATE


def _conv_silu_rows(p_ref, cw_ref, cb_ref, out_ref, seq):
    n = p_ref.shape[0]
    row = lax.broadcasted_iota(jnp.int32, (ROW_BLOCK, 1), 0)
    w_prev, w_cur, w_next, bias = cw_ref[0:1], cw_ref[1:2], cw_ref[2:3], cb_ref[...]
    for r in range(0, n, ROW_BLOCK):
        lo, hi = max(r - SUBLANES, 0), min(r + ROW_BLOCK + SUBLANES, n)
        blk = p_ref[lo:hi]
        cur = blk[r - lo:r - lo + ROW_BLOCK]
        prev = pltpu.roll(blk, 1, axis=0)[r - lo:r - lo + ROW_BLOCK]
        nxt = pltpu.roll(blk, hi - lo - 1, axis=0)[r - lo:r - lo + ROW_BLOCK]
        if r % seq == 0:
            prev = jnp.where(row == 0, 0.0, prev)
        if (r + ROW_BLOCK) % seq == 0:
            nxt = jnp.where(row == ROW_BLOCK - 1, 0.0, nxt)
        out_ref[r:r + ROW_BLOCK] = _silu(prev * w_prev + cur * w_cur + nxt * w_next + bias)


def _odd_in_kernel(h_ref, mods_ref, g_ref, w_ref, wdt_ref, cw_ref, cb_ref, dtb_ref,
                   z_ref, xbc_ref, dt_ref, u_ref, p_ref, *, seq):
    j = pl.program_id(1)
    last = pl.num_programs(1) - 1

    def project():
        return jnp.dot(u_ref[...], w_ref[...], preferred_element_type=F32)

    @pl.when(j == 0)
    def _():
        _modulate_rows(h_ref, g_ref[2:3], mods_ref[3:4], mods_ref[4:5], u_ref)
        raw = jnp.dot(u_ref[...], wdt_ref[...], preferred_element_type=F32) + dtb_ref[...]
        dt_ref[...] = jnp.maximum(raw, 0.0) + jnp.log1p(jnp.exp(-jnp.abs(raw)))

    @pl.when(j < N_Z)
    def _():
        z_ref[...] = project()

    @pl.when(j == N_Z)
    def _():
        p_ref[N_Z % 2] = project()

    for parity in range(2):
        @pl.when((j > N_Z) & (j < last) & (j % 2 == parity))
        def _(parity=parity):
            _conv_silu_rows(p_ref.at[1 - parity], cw_ref, cb_ref, xbc_ref, seq)
            p_ref[parity] = project()

    @pl.when(j == last)
    def _():
        _conv_silu_rows(p_ref.at[(N_Z + N_XBC - 1) % 2], cw_ref, cb_ref, xbc_ref, seq)


def _odd_in(h, mods, norm_g, w_in, conv_w, conv_b, dt_bias, cond, *, o):
    tokens = h.shape[0]
    seq = cond.seq
    tm, tn = 1024, GROUP_WIDTH
    assert tm % seq == 0 and tokens % tm == 0
    dt_block = (D_INNER + CONV_DIM) // LANES
    n_chunks = N_Z + N_XBC
    conv_idx = lambda j: jnp.maximum(j - 1 - N_Z, 0)
    return pl.pallas_call(
        functools.partial(_odd_in_kernel, seq=seq),
        grid=(tokens // tm, n_chunks + 1),
        in_specs=[
            pl.BlockSpec((tm, D_MODEL), lambda i, j: (i, 0)),
            *cond.specs(tm),
            pl.BlockSpec((None, D_MODEL, tn), lambda i, j: (o, 0, jnp.minimum(j, n_chunks - 1))),
            pl.BlockSpec((None, D_MODEL, 2 * SSM_HEADS), lambda i, j: (o, 0, dt_block)),
            pl.BlockSpec((None, 3, tn), lambda i, j: (o, 0, conv_idx(j))),
            pl.BlockSpec((None, 1, tn), lambda i, j: (o, 0, conv_idx(j))),
            pl.BlockSpec((None, 1, 2 * SSM_HEADS), lambda i, j: (o, 0, 0)),
        ],
        out_specs=[
            pl.BlockSpec((None, tm, tn), lambda i, j: (jnp.minimum(j, N_Z - 1), i, 0)),
            pl.BlockSpec((None, tm, tn), lambda i, j: (conv_idx(j), i, 0)),
            pl.BlockSpec((tm, 2 * SSM_HEADS), lambda i, j: (i, 0)),
        ],
        out_shape=[
            jax.ShapeDtypeStruct((N_Z, tokens, tn), F32),
            jax.ShapeDtypeStruct((N_XBC, tokens, tn), F32),
            jax.ShapeDtypeStruct((tokens, 2 * SSM_HEADS), F32),
        ],
        scratch_shapes=[pltpu.VMEM((tm, D_MODEL), BF16), pltpu.VMEM((2, tm, tn), F32)],
        compiler_params=_params(("parallel", "arbitrary")),
        name="odd_in",
    )(h, mods, norm_g, w_in, w_in, conv_w, conv_b, dt_bias)


def _split3(x):
    x1 = x.astype(BF16)
    r1 = x - x1.astype(F32)
    x2 = r1.astype(BF16)
    x3 = (r1 - x2.astype(F32)).astype(BF16)
    return x1, x2, x3


def _ssd_kernel(*refs, direction, has_h0, emit_state, combine):
    refs = list(refs)
    x_ref, b_ref, c_ref, dt_ref, alog_ref = refs[:5]
    del refs[:5]
    h0_ref = refs.pop(0) if has_h0 else None
    if combine:
        yf_ref, z_ref, dskip_ref, ng_ref = refs[:4]
        del refs[:4]
        if emit_state:
            del refs[0]
    y_ref = refs.pop(0)
    hT_out_ref = refs.pop(0) if emit_state else None
    ht_ref = refs.pop(0)
    yb_ref = refs.pop(0) if combine else None

    c = pl.program_id(1)
    q = SSM_CHUNK
    pairs = GROUP_WIDTH // LANES

    @pl.when(c == 0)
    def _():
        if has_h0:
            for g in range(SSM_GROUPS):
                for pr in range(pairs):
                    r0 = g * GROUP_WIDTH + pr * LANES
                    ht_ref[g, :, pr * LANES:(pr + 1) * LANES] = h0_ref[r0:r0 + LANES, :].T
        else:
            ht_ref[...] = jnp.zeros_like(ht_ref)

    ii = lax.broadcasted_iota(jnp.int32, (q, q), 0)
    jj = lax.broadcasted_iota(jnp.int32, (q, q), 1)
    causal = (jj <= ii) if direction == 0 else (jj >= ii)
    tri = jnp.where(causal, 1.0, 0.0).astype(BF16)
    low_lane = lax.broadcasted_iota(jnp.int32, (1, LANES), 1) < SSM_HEADDIM

    dt = dt_ref[...]
    dta = dt * (-jnp.exp(alog_ref[...]))
    a = sum(jnp.dot(tri, part, preferred_element_type=F32) for part in _split3(dta))
    a_end = a[q - 1:q] if direction == 0 else a[0:1]
    a_t = a.T
    dt_t = dt.T
    s_t = (dt * jnp.exp(a_end - a)).T
    end_scale = jnp.exp(a_end)

    for g in range(SSM_GROUPS):
        bc_lanes = slice((g % GROUPS_PER_CHUNK) * D_STATE, (g % GROUPS_PER_CHUNK + 1) * D_STATE)
        bg_t = b_ref[g // GROUPS_PER_CHUNK, :, bc_lanes].T
        cg32 = c_ref[g // GROUPS_PER_CHUNK, :, bc_lanes]
        cb = jnp.dot(cg32.astype(BF16), bg_t.astype(BF16), preferred_element_type=F32)
        for pr in range(pairs):
            lanes = slice(pr * LANES, (pr + 1) * LANES)
            xp = x_ref[g, :, lanes].astype(BF16)
            hp = ht_ref[g, :, lanes]
            hpb = hp.astype(BF16)
            ys, ds, scales = [], [], []
            for e in range(2):
                hl = direction * SSM_HEADS + g * HEADS_PER_GROUP + 2 * pr + e
                col = jnp.broadcast_to(a[:, hl:hl + 1], (q, q))
                decay = jnp.exp(jnp.where(causal, col - a_t[hl:hl + 1, :], -jnp.inf))
                w = (cb * decay * dt_t[hl:hl + 1, :]).astype(BF16)
                cdec = (cg32 * jnp.exp(col)).astype(BF16)
                ys.append(jnp.dot(w, xp, preferred_element_type=F32)
                          + jnp.dot(cdec, hpb, preferred_element_type=F32))
                bs = (bg_t * s_t[hl:hl + 1, :]).astype(BF16)
                ds.append(jnp.dot(bs, xp, preferred_element_type=F32))
                scales.append(jnp.broadcast_to(end_scale[:, hl:hl + 1], (1, LANES)))
            y_pair = jnp.where(low_lane, ys[0], ys[1])
            if combine:
                yb_ref[g, :, lanes] = y_pair
            else:
                y_ref[g, :, lanes] = y_pair
            ht_ref[g, :, lanes] = (hp * jnp.where(low_lane, scales[0], scales[1])
                                   + jnp.where(low_lane, ds[0], ds[1]))

    if combine:
        ssq = jnp.zeros((q, 1), F32)
        for g in range(SSM_GROUPS):
            xg = x_ref[g]
            y = (yf_ref[g] + dskip_ref[0, g] * xg) + (yb_ref[g] + dskip_ref[1, g] * xg)
            y = y * _silu(z_ref[g])
            yb_ref[g] = y
            ssq = ssq + jnp.sum(y * y, axis=-1, keepdims=True)
        inv = lax.rsqrt(ssq / D_INNER + EPS)
        for g in range(SSM_GROUPS):
            y_ref[g] = (yb_ref[g] * inv * ng_ref[g]).astype(BF16)

    if emit_state:
        @pl.when(c == pl.num_programs(1) - 1)
        def _():
            for g in range(SSM_GROUPS):
                for pr in range(pairs):
                    r0 = g * GROUP_WIDTH + pr * LANES
                    hT_out_ref[r0:r0 + LANES, :] = ht_ref[g, :, pr * LANES:(pr + 1) * LANES].T


def _ssd(xbc, dt, a_log, h0, *, o, direction, batch, seq, final_state=None, combine=None):
    tokens = xbc.shape[1]
    q = SSM_CHUNK
    nc = seq // q
    if direction == 0:
        chunk = lambda b, c: b * nc + c
    else:
        chunk = lambda b, c: b * nc + (nc - 1 - c)
    rows = lambda n, first: pl.BlockSpec((n, q, GROUP_WIDTH), lambda b, c: (first // n, chunk(b, c), 0))
    in_specs = [rows(X_CHUNKS, 0), rows(BC_CHUNKS, X_CHUNKS), rows(BC_CHUNKS, X_CHUNKS + BC_CHUNKS),
                pl.BlockSpec((q, 2 * SSM_HEADS), lambda b, c: (chunk(b, c), 0)),
                pl.BlockSpec((None, 1, 2 * SSM_HEADS), lambda b, c: (o, 0, 0))]
    args = [xbc, xbc, xbc, dt, a_log]
    if h0 is not None:
        in_specs.append(pl.BlockSpec((None, None, None, D_INNER, D_STATE), lambda b, c: (b, o, direction, 0, 0)))
        args.append(h0)
    if combine is not None:
        y_f, z, d_skip, norm_g = combine
        in_specs += [rows(X_CHUNKS, 0), rows(N_Z, 0),
                     pl.BlockSpec((None, 2, SSM_GROUPS, 1, GROUP_WIDTH), lambda b, c: (o, 0, 0, 0, 0)),
                     pl.BlockSpec((None, SSM_GROUPS, 1, GROUP_WIDTH), lambda b, c: (o, 0, 0, 0))]
        args += [y_f, z, d_skip, norm_g]
    out_specs = [rows(X_CHUNKS, 0)]
    out_shape = [jax.ShapeDtypeStruct((X_CHUNKS, tokens, GROUP_WIDTH), BF16 if combine is not None else F32)]
    aliases = {}
    emit_state = final_state is not None
    if emit_state:
        out_specs.append(pl.BlockSpec((None, None, D_INNER, D_STATE), lambda b, c: (b, direction, 0, 0)))
        out_shape.append(jax.ShapeDtypeStruct((batch, 2, D_INNER, D_STATE), F32))
        if not isinstance(final_state, str):
            assert combine is not None
            in_specs.append(pl.BlockSpec(memory_space=pl.ANY))
            aliases = {len(args): 1}
            args.append(final_state)
    scratch = [pltpu.VMEM((SSM_GROUPS, D_STATE, GROUP_WIDTH), F32)]
    if combine is not None:
        scratch.append(pltpu.VMEM((SSM_GROUPS, q, GROUP_WIDTH), F32))
    return pl.pallas_call(
        functools.partial(_ssd_kernel, direction=direction, has_h0=h0 is not None,
                          emit_state=emit_state, combine=combine is not None),
        grid=(batch, nc),
        in_specs=in_specs,
        out_specs=out_specs,
        out_shape=out_shape,
        scratch_shapes=scratch,
        input_output_aliases=aliases,
        compiler_params=_params(("parallel", "arbitrary")),
        name="ssd_fwd" if direction == 0 else "ssd_bwd",
    )(*args)


def _trunk(h, mods, w, *, row0, n_cond, batch, seq, cache_k=None, cache_v=None, state=None):
    context = cache_k is None
    outs = {}
    for l in range(DEPTH):
        cond = _Cond(l, row0, n_cond, seq)
        norm_g = w["norm_g"]
        h = _ffn(h, mods, norm_g, w["ffn_w_in"], w["ffn_w_out"], cond, half=0)
        if l % 2 == 0:
            e = l // 2
            pool_y, q, k, v = _even_in(h, mods, norm_g, w["mix_w_in"], w["pool_w"], w["pool_scale"],
                                       w["qk_norm_g"], cond, e=e, rope=not context)
            attn = _attention(q, k, v, cache_k, cache_v, e=e, batch=batch, seq=seq)
            h = _even_out(pool_y, attn, w["mix_w_out"], h, mods, norm_g, cond, e=e)
            outs["k"], outs["v"] = k, v
        else:
            o = l // 2
            z, xbc, dt = _odd_in(h, mods, norm_g, w["ssm_w_in"], w["ssm_conv_w"], w["ssm_conv_b"],
                                 w["ssm_dt_bias"], cond, o=o)
            scan = functools.partial(_ssd, xbc, dt, w["ssm_A_log"], state, o=o, batch=batch, seq=seq)
            fwd = scan(direction=0, final_state="new" if context else None)
            bwd = scan(direction=1, final_state=fwd[1] if context else None,
                       combine=(fwd[0], z, w["ssm_D"], w["ssm_norm_g"]))
            if context:
                outs["ssm"] = bwd[1]
            h = _odd_out(bwd[0], w["ssm_w_out"], h, mods, norm_g, cond, o=o)
        h = _ffn(h, mods, norm_g, w["ffn_w_in"], w["ffn_w_out"], cond, half=1)
    return h, outs


def kernel(x_prompt, x_sample, cache_k, cache_v, state_ssm, c, c_ctx, ada_w, ada_b, norm_g, ffn_w_in,
           ffn_w_out, mix_w_in, pool_w, pool_scale, qk_norm_g, mix_w_out, ssm_w_in, ssm_conv_w,
           ssm_conv_b, ssm_dt_bias, ssm_A_log, ssm_D, ssm_norm_g, ssm_w_out):
    batch, seq, _ = x_prompt.shape
    dec_batch, dec_seq, _ = x_sample.shape
    n_even, n_odd = mix_w_in.shape[0], ssm_w_in.shape[0]
    past = cache_k.shape[2]

    cond = jnp.zeros((COND_ROWS, D_MODEL), F32).at[0].set(c_ctx).at[1:1 + dec_batch].set(c)
    mods = _ada_mods(cond, ada_w, ada_b).reshape(DEPTH, COND_ROWS, N_MOD, D_MODEL)

    w = dict(
        norm_g=norm_g,
        ffn_w_in=_interleave_gate_columns(ffn_w_in.astype(BF16)), ffn_w_out=ffn_w_out.astype(BF16),
        mix_w_in=mix_w_in.astype(BF16), pool_w=pool_w.astype(BF16),
        pool_scale=pool_scale.reshape(n_even, 1, POOL_WIDTH), qk_norm_g=qk_norm_g,
        mix_w_out=mix_w_out.astype(BF16),
        ssm_w_in=ssm_w_in.astype(BF16),
        ssm_conv_w=jnp.swapaxes(ssm_conv_w, 1, 2), ssm_conv_b=ssm_conv_b.reshape(n_odd, 1, CONV_DIM),
        ssm_dt_bias=ssm_dt_bias.reshape(n_odd, 1, 2 * SSM_HEADS),
        ssm_A_log=ssm_A_log.reshape(n_odd, 1, 2 * SSM_HEADS),
        ssm_D=jnp.repeat(ssm_D, SSM_HEADDIM, axis=-1).reshape(n_odd, 2, SSM_GROUPS, 1, GROUP_WIDTH),
        ssm_norm_g=ssm_norm_g.reshape(n_odd, SSM_GROUPS, 1, GROUP_WIDTH),
        ssm_w_out=ssm_w_out.astype(BF16),
    )

    y_prompt, ctx = _trunk(x_prompt.reshape(batch * seq, D_MODEL), mods, w, row0=0, n_cond=1,
                           batch=batch, seq=seq)
    y_sample, _ = _trunk(x_sample.reshape(dec_batch * dec_seq, D_MODEL), mods, w, row0=1, n_cond=dec_batch,
                         batch=dec_batch, seq=dec_seq,
                         cache_k=cache_k.reshape(dec_batch, n_even, past, KV_WIDTH),
                         cache_v=cache_v.reshape(dec_batch, n_even, past, KV_WIDTH),
                         state=state_ssm.reshape(dec_batch, n_odd, 2, D_INNER, D_STATE))

    new_k = ctx["k"].reshape(batch, n_even, seq, N_KV_HEADS, HEAD_DIM)
    new_v = ctx["v"].reshape(batch, n_even, seq, N_KV_HEADS, HEAD_DIM)
    new_ssm = ctx["ssm"].reshape(batch, n_odd, 2, SSM_HEADS, SSM_HEADDIM, D_STATE)
    return (y_prompt.reshape(batch, seq, D_MODEL), y_sample.reshape(dec_batch, dec_seq, D_MODEL),
            new_k, new_v, new_ssm)
```

```python
import functools

import jax
import jax.numpy as jnp
from jax import lax
from jax.experimental import pallas as pl
from jax.experimental.pallas import tpu as pltpu

F32 = jnp.float32
BF16 = jnp.bfloat16

D_MODEL = 2048
DEPTH = 2
GRID_W = 64
EPS = 1e-6
N_MOD = 9
N_NORM = 6
D_FF = 5632
POOL_WINDOWS = (2, 4, 8, 16)
POOL_WIDTH = 1024
POOL_GROUP_DIM = 256
HEAD_DIM = 128
N_KV_HEADS = 2
Q_PER_KV = 4
ATTN_WIDTH = 1024
KV_WIDTH = 256
MIX_IN = 2560
ROPE_THETA = 10000.0
D_INNER = 4096
SSM_HEADDIM = 64
SSM_HEADS = 64
SSM_GROUPS = 8
HEADS_PER_GROUP = 8
D_STATE = 128
SSM_CHUNK = 128
CONV_DIM = 6144
GROUP_WIDTH = HEADS_PER_GROUP * SSM_HEADDIM

LANES = 128
VMEM_LIMIT = 56 * 1024 * 1024
COND_ROWS = 16


def _params(sem):
    return pltpu.CompilerParams(dimension_semantics=sem, vmem_limit_bytes=VMEM_LIMIT)


def _rms(x, g):
    ms = jnp.mean(x * x, axis=-1, keepdims=True)
    return x * lax.rsqrt(ms + EPS) * g


def _silu(x):
    return x * jax.nn.sigmoid(x)


ROW_BLOCK = 16


def _modulate_rows(h_ref, g, shift, scale, u_ref):
    gain = g * (1 + scale)
    for r in range(0, h_ref.shape[0], ROW_BLOCK):
        x = h_ref[r:r + ROW_BLOCK]
        ms = jnp.mean(x * x, axis=-1, keepdims=True)
        u_ref[r:r + ROW_BLOCK] = (x * lax.rsqrt(ms + EPS) * gain + shift).astype(BF16)


def _residual_rows(h_ref, y_ref, g, gate, out_ref):
    gain = g * gate
    for r in range(0, h_ref.shape[0], ROW_BLOCK):
        y = y_ref[r:r + ROW_BLOCK]
        ms = jnp.mean(y * y, axis=-1, keepdims=True)
        out_ref[r:r + ROW_BLOCK] = h_ref[r:r + ROW_BLOCK] + y * lax.rsqrt(ms + EPS) * gain


class _Cond:
    def __init__(self, layer, row0, n_cond, seq):
        self.layer, self.row0, self.n_cond, self.seq = layer, row0, n_cond, seq

    def specs(self, tm):
        layer, row0 = self.layer, self.row0
        if self.n_cond == 1:
            row = lambda i: row0
        else:
            assert self.seq % tm == 0
            per_seq = self.seq // tm
            row = lambda i: row0 + i // per_seq
        return [pl.BlockSpec((None, None, N_MOD, D_MODEL), lambda i, *_: (layer, row(i), 0, 0)),
                pl.BlockSpec((None, N_NORM, D_MODEL), lambda *_: (layer, 0, 0))]


def _mods_kernel(cond_ref, w_ref, b_ref, out_ref):
    sc = _silu(cond_ref[...]).astype(BF16)
    w = w_ref[...].astype(BF16)
    out_ref[...] = jnp.dot(sc, w, preferred_element_type=F32) + b_ref[...]


def _ada_mods(cond, ada_w, ada_b):
    tn = 1024
    n_out = N_MOD * D_MODEL
    return pl.pallas_call(
        _mods_kernel,
        grid=(DEPTH, n_out // tn),
        in_specs=[
            pl.BlockSpec((COND_ROWS, D_MODEL), lambda l, n: (0, 0)),
            pl.BlockSpec((None, D_MODEL, tn), lambda l, n: (l, 0, n)),
            pl.BlockSpec((None, 1, tn), lambda l, n: (l, 0, n)),
        ],
        out_specs=pl.BlockSpec((None, COND_ROWS, tn), lambda l, n: (l, 0, n)),
        out_shape=jax.ShapeDtypeStruct((DEPTH, COND_ROWS, n_out), F32),
        compiler_params=_params(("parallel", "parallel")),
        name="ada_mods",
    )(cond, ada_w, ada_b.reshape(DEPTH, 1, n_out))


FFN_SUB = 256


def _ffn_kernel(h_ref, mods_ref, g_ref, wa_ref, wb_ref, wo_ref, out_ref, u_ref, acc_ref, ab_ref, hid_ref,
                *, mod0, g0):
    j = pl.program_id(1)
    n_sub = wo_ref.shape[0] // FFN_SUB

    def gate_rows(s):
        for r in range(0, ab_ref.shape[1], ROW_BLOCK):
            ab = ab_ref[s, r:r + ROW_BLOCK]
            hid_ref[s, r:r + ROW_BLOCK] = (_silu(ab[:, :FFN_SUB]) * ab[:, FFN_SUB:]).astype(BF16)

    def partial_out():
        u = u_ref[...]
        for s in range(n_sub):
            cols = slice(s * FFN_SUB, (s + 1) * FFN_SUB)
            w_ab = jnp.concatenate([wa_ref[:, cols], wb_ref[:, cols]], axis=1)
            ab_ref[s] = jnp.dot(u, w_ab, preferred_element_type=F32)
            gate_rows(s)
        y = jnp.dot(hid_ref[0], wo_ref[:FFN_SUB], preferred_element_type=F32)
        for s in range(1, n_sub):
            y = y + jnp.dot(hid_ref[s], wo_ref[s * FFN_SUB:(s + 1) * FFN_SUB], preferred_element_type=F32)
        return y

    @pl.when(j == 0)
    def _():
        _modulate_rows(h_ref, g_ref[g0:g0 + 1], mods_ref[mod0:mod0 + 1], mods_ref[mod0 + 1:mod0 + 2], u_ref)
        acc_ref[...] = partial_out()

    @pl.when(j > 0)
    def _():
        acc_ref[...] += partial_out()

    @pl.when(j == pl.num_programs(1) - 1)
    def _():
        _residual_rows(h_ref, acc_ref, g_ref[g0 + 1:g0 + 2], 0.5 * mods_ref[mod0 + 2:mod0 + 3], out_ref)


def _ffn(h, mods, norm_g, w_in, w_out, cond, *, half):
    tokens = h.shape[0]
    tm, tf = 512, 512
    nf = D_FF // tf
    layer = cond.layer
    return pl.pallas_call(
        functools.partial(_ffn_kernel, mod0=6 * half, g0=4 * half),
        grid=(tokens // tm, nf),
        in_specs=[
            pl.BlockSpec((tm, D_MODEL), lambda i, j: (i, 0)),
            *cond.specs(tm),
            pl.BlockSpec((None, None, D_MODEL, tf), lambda i, j: (layer, half, 0, j)),
            pl.BlockSpec((None, None, D_MODEL, tf), lambda i, j: (layer, half, 0, nf + j)),
            pl.BlockSpec((None, None, tf, D_MODEL), lambda i, j: (layer, half, j, 0)),
        ],
        out_specs=pl.BlockSpec((tm, D_MODEL), lambda i, j: (i, 0)),
        out_shape=jax.ShapeDtypeStruct((tokens, D_MODEL), F32),
        scratch_shapes=[pltpu.VMEM((tm, D_MODEL), BF16), pltpu.VMEM((tm, D_MODEL), F32),
                        pltpu.VMEM((tf // FFN_SUB, tm, 2 * FFN_SUB), F32),
                        pltpu.VMEM((tf // FFN_SUB, tm, FFN_SUB), BF16)],
        compiler_params=_params(("parallel", "arbitrary")),
        name="ffn",
    )(h, mods, norm_g, w_in, w_in, w_out)


SUBLANES = 8


POOL_ROWS = 64


def _halo_block(p_ref, r, n_rows, cols, seq):
    zeros = jnp.zeros((SUBLANES, cols.stop - cols.start), F32)
    before = zeros if r % seq == 0 else p_ref[r - SUBLANES:r, cols]
    after = zeros if (r + n_rows) % seq == 0 else p_ref[r + n_rows:r + n_rows + SUBLANES, cols]
    return jnp.concatenate([before, p_ref[r:r + n_rows, cols], after], axis=0)


def _pool_delta_rows(p_ref, cols, window, seq, d_ref):
    half = window // 2
    assert half <= SUBLANES and seq % POOL_ROWS == 0
    n = p_ref.shape[0]
    rows = POOL_ROWS + 2 * SUBLANES
    row = lax.broadcasted_iota(jnp.int32, (POOL_ROWS, 1), 0)
    for r in range(0, n, POOL_ROWS):
        blk = _halo_block(p_ref, r, POOL_ROWS, cols, seq)
        fwd = blk
        bwd = blk
        m = 1
        while m < half:
            fwd = fwd + pltpu.roll(fwd, rows - m, axis=0)
            bwd = bwd + pltpu.roll(bwd, m, axis=0)
            m *= 2
        total = (fwd + pltpu.roll(bwd, 1, axis=0))[SUBLANES:SUBLANES + POOL_ROWS]
        pos = row + r % seq
        count = jnp.minimum(pos + half, seq) - jnp.maximum(pos - half, 0)
        d = total / count.astype(F32) - blk[SUBLANES:SUBLANES + POOL_ROWS]
        d_ref[r:r + POOL_ROWS, cols] = d.astype(BF16)


def _rope(x, cos, sin_lo, sin_hi):
    quarter = HEAD_DIM // 4
    return (x * cos + pltpu.roll(x, HEAD_DIM - quarter, axis=1) * sin_lo
            + pltpu.roll(x, quarter, axis=1) * sin_hi)


HEAD_ROWS_ROPE = 256


def _even_in_kernel(h_ref, mods_ref, g_ref, w_ref, pw_ref, ps_ref, qkg_ref, *rest, seq, rope):
    if rope:
        cos_ref, slo_ref, shi_ref, pool_ref, q_ref, k_ref, v_ref, u_ref, p_ref, d_ref = rest
    else:
        pool_ref, q_ref, k_ref, v_ref, u_ref, p_ref, d_ref = rest
    j = pl.program_id(1)
    tm = h_ref.shape[0]
    n_pool = POOL_WIDTH // p_ref.shape[2]
    n_q = ATTN_WIDTH // p_ref.shape[2]

    def project(chunk):
        p_ref[chunk % 2] = jnp.dot(u_ref[...], w_ref[...], preferred_element_type=F32)

    groups = p_ref.shape[2] // POOL_GROUP_DIM

    def pool_deltas(chunk):
        for gi in range(groups):
            cols = slice(gi * POOL_GROUP_DIM, (gi + 1) * POOL_GROUP_DIM)
            _pool_delta_rows(p_ref.at[chunk % 2], cols, POOL_WINDOWS[groups * chunk + gi], seq, d_ref)

    def pool_mix():
        for gi in range(groups):
            cols = slice(gi * POOL_GROUP_DIM, (gi + 1) * POOL_GROUP_DIM)
            y = jnp.dot(d_ref[:, cols], pw_ref[gi], preferred_element_type=F32)
            pool_ref[:, cols] = (y * ps_ref[:, cols]).astype(BF16)

    def heads_epilogue(chunk, n_heads, gain, dst_ref):
        src = p_ref.at[chunk % 2]
        step = HEAD_ROWS_ROPE if rope else ROW_BLOCK
        for r in range(0, tm, step):
            rows = slice(r, r + step)
            for hd in range(n_heads):
                cols = slice(hd * HEAD_DIM, (hd + 1) * HEAD_DIM)
                x = _rms(src[rows, cols], gain)
                if rope:
                    x = _rope(x, cos_ref[rows], slo_ref[rows], shi_ref[rows])
                dst_ref[rows, cols] = x.astype(dst_ref.dtype)

    def kv_epilogue(chunk):
        heads_epilogue(chunk, N_KV_HEADS, qkg_ref[1:2], k_ref)
        src = p_ref.at[chunk % 2]
        for r in range(0, tm, ROW_BLOCK):
            v_ref[r:r + ROW_BLOCK] = src[r:r + ROW_BLOCK, KV_WIDTH:]

    @pl.when(j == 0)
    def _():
        _modulate_rows(h_ref, g_ref[2:3], mods_ref[3:4], mods_ref[4:5], u_ref)
        project(0)

    for chunk in range(1, n_pool + n_q + 2):
        @pl.when(j == chunk)
        def _(chunk=chunk):
            done = chunk - 1
            if done < n_pool:
                pool_deltas(done)
            elif done < n_pool + n_q:
                heads_epilogue(done, Q_PER_KV, qkg_ref[0:1], q_ref)
            else:
                kv_epilogue(done)
            if chunk <= n_pool + n_q:
                project(chunk)
            if done < n_pool:
                pool_mix()


def _rope_tables(seq):
    pos = jnp.arange(seq)
    row = (pos // GRID_W).astype(F32)
    col = (pos % GRID_W).astype(F32)
    quarter = HEAD_DIM // 4
    inv_freq = ROPE_THETA ** (-jnp.arange(quarter, dtype=F32) / quarter)
    ang_r = row[:, None] * inv_freq[None]
    ang_c = col[:, None] * inv_freq[None]
    zero = jnp.zeros_like(ang_r)
    cos = jnp.concatenate([jnp.cos(ang_r), jnp.cos(ang_r), jnp.cos(ang_c), jnp.cos(ang_c)], axis=-1)
    sin_lo = jnp.concatenate([-jnp.sin(ang_r), zero, -jnp.sin(ang_c), zero], axis=-1)
    sin_hi = jnp.concatenate([zero, jnp.sin(ang_r), zero, jnp.sin(ang_c)], axis=-1)
    return cos, sin_lo, sin_hi


def _even_in(h, mods, norm_g, w_in, pool_w, pool_scale, qk_g, cond, *, e, rope):
    tokens = h.shape[0]
    seq = cond.seq
    tm, tn = 1024, 512
    assert tm % seq == 0 and tokens % tm == 0
    n_pool = POOL_WIDTH // tn
    n_q = ATTN_WIDTH // tn
    n_chunks = MIX_IN // tn
    pool_idx = lambda j: jnp.clip(j - 1, 0, n_pool - 1)
    q_idx = lambda j: jnp.clip(j - 1 - n_pool, 0, n_q - 1)
    in_specs = [
        pl.BlockSpec((tm, D_MODEL), lambda i, j: (i, 0)),
        *cond.specs(tm),
        pl.BlockSpec((None, D_MODEL, tn), lambda i, j: (e, 0, jnp.minimum(j, n_chunks - 1))),
        pl.BlockSpec((None, 2, POOL_GROUP_DIM, POOL_GROUP_DIM), lambda i, j: (e, pool_idx(j), 0, 0)),
        pl.BlockSpec((None, 1, tn), lambda i, j: (e, 0, pool_idx(j))),
        pl.BlockSpec((None, 2, HEAD_DIM), lambda i, j: (e, 0, 0)),
    ]
    args = [h, mods, norm_g, w_in, pool_w, pool_scale, qk_g]
    if rope:
        assert tm == seq
        in_specs += [pl.BlockSpec((tm, HEAD_DIM), lambda i, j: (0, 0))] * 3
        args += list(_rope_tables(seq))
    return pl.pallas_call(
        functools.partial(_even_in_kernel, seq=seq, rope=rope),
        grid=(tokens // tm, n_chunks + 1),
        in_specs=in_specs,
        out_specs=[
            pl.BlockSpec((tm, tn), lambda i, j: (i, pool_idx(j))),
            pl.BlockSpec((tm, tn), lambda i, j: (i, q_idx(j))),
            pl.BlockSpec((tm, KV_WIDTH), lambda i, j: (i, 0)),
            pl.BlockSpec((tm, KV_WIDTH), lambda i, j: (i, 0)),
        ],
        out_shape=[
            jax.ShapeDtypeStruct((tokens, POOL_WIDTH), BF16),
            jax.ShapeDtypeStruct((tokens, ATTN_WIDTH), BF16),
            jax.ShapeDtypeStruct((tokens, KV_WIDTH), F32),
            jax.ShapeDtypeStruct((tokens, KV_WIDTH), F32),
        ],
        scratch_shapes=[pltpu.VMEM((tm, D_MODEL), BF16), pltpu.VMEM((2, tm, tn), F32),
                        pltpu.VMEM((tm, tn), BF16)],
        compiler_params=_params(("parallel", "arbitrary")),
        name="even_in",
    )(*args)


def _attn_kernel(q_ref, k_ref, v_ref, *rest, cached):
    if cached:
        ck_ref, cv_ref, o_ref = rest
    else:
        (o_ref,) = rest
    scale = HEAD_DIM ** -0.5
    nt = (((1,), (1,)), ((), ()))
    kb = k_ref[...].astype(BF16)
    vb = v_ref[...].astype(BF16)
    if cached:
        ckb = ck_ref[...].astype(BF16)
        cvb = cv_ref[...].astype(BF16)
    for hd in range(Q_PER_KV):
        sl = slice(hd * HEAD_DIM, (hd + 1) * HEAD_DIM)
        q = q_ref[:, sl]
        s = lax.dot_general(q, kb, nt, preferred_element_type=F32) * scale
        m = jnp.max(s, axis=-1, keepdims=True)
        if cached:
            sc = lax.dot_general(q, ckb, nt, preferred_element_type=F32) * scale
            m = jnp.maximum(m, jnp.max(sc, axis=-1, keepdims=True))
        e = jnp.exp(s - m)
        denom = jnp.sum(e, axis=-1, keepdims=True)
        o = jnp.dot(e.astype(BF16), vb, preferred_element_type=F32)
        if cached:
            ec = jnp.exp(sc - m)
            denom = denom + jnp.sum(ec, axis=-1, keepdims=True)
            o = o + jnp.dot(ec.astype(BF16), cvb, preferred_element_type=F32)
        o_ref[:, sl] = (o / denom).astype(BF16)


def _attention(q, k, v, cache_k, cache_v, *, e, batch, seq):
    tokens = q.shape[0]
    tq = 256
    nq = seq // tq
    gw = Q_PER_KV * HEAD_DIM
    cached = cache_k is not None
    in_specs = [
        pl.BlockSpec((tq, gw), lambda b, g, i: (b * nq + i, g)),
        pl.BlockSpec((seq, HEAD_DIM), lambda b, g, i: (b, g)),
        pl.BlockSpec((seq, HEAD_DIM), lambda b, g, i: (b, g)),
    ]
    args = [q, k, v]
    if cached:
        past = cache_k.shape[2]
        in_specs += [pl.BlockSpec((None, None, past, HEAD_DIM), lambda b, g, i: (b, e, 0, g))] * 2
        args += [cache_k, cache_v]
    return pl.pallas_call(
        functools.partial(_attn_kernel, cached=cached),
        grid=(batch, N_KV_HEADS, nq),
        in_specs=in_specs,
        out_specs=pl.BlockSpec((tq, gw), lambda b, g, i: (b * nq + i, g)),
        out_shape=jax.ShapeDtypeStruct((tokens, ATTN_WIDTH), BF16),
        compiler_params=_params(("parallel", "parallel", "parallel")),
        name="attention",
    )(*args)


def _even_out_kernel(pool_ref, attn_ref, w_ref, h_ref, mods_ref, g_ref, out_ref):
    y = jnp.dot(pool_ref[...], w_ref[:POOL_WIDTH], preferred_element_type=F32)
    y = y + jnp.dot(attn_ref[...], w_ref[POOL_WIDTH:], preferred_element_type=F32)
    out_ref[...] = y
    _residual_rows(h_ref, out_ref, g_ref[3:4], mods_ref[5:6], out_ref)


def _even_out(pool_y, attn, w_out, h, mods, norm_g, cond, *, e):
    tokens = h.shape[0]
    tm = 512
    return pl.pallas_call(
        _even_out_kernel,
        grid=(tokens // tm,),
        in_specs=[
            pl.BlockSpec((tm, POOL_WIDTH), lambda i: (i, 0)),
            pl.BlockSpec((tm, ATTN_WIDTH), lambda i: (i, 0)),
            pl.BlockSpec((None, POOL_WIDTH + ATTN_WIDTH, D_MODEL), lambda i: (e, 0, 0)),
            pl.BlockSpec((tm, D_MODEL), lambda i: (i, 0)),
            *cond.specs(tm),
        ],
        out_specs=pl.BlockSpec((tm, D_MODEL), lambda i: (i, 0)),
        out_shape=jax.ShapeDtypeStruct((tokens, D_MODEL), F32),
        compiler_params=_params(("parallel",)),
        name="even_out",
    )(pool_y, attn, w_out, h, mods, norm_g)


def _odd_out_kernel(x_ref, w_ref, h_ref, mods_ref, g_ref, out_ref):
    k = pl.program_id(1)
    y = jnp.dot(x_ref[0], w_ref[:GROUP_WIDTH], preferred_element_type=F32)
    for g in range(1, x_ref.shape[0]):
        y = y + jnp.dot(x_ref[g], w_ref[g * GROUP_WIDTH:(g + 1) * GROUP_WIDTH], preferred_element_type=F32)

    @pl.when(k == 0)
    def _():
        out_ref[...] = y

    @pl.when(k == pl.num_programs(1) - 1)
    def _():
        out_ref[...] += y
        _residual_rows(h_ref, out_ref, g_ref[3:4], mods_ref[5:6], out_ref)


def _odd_out(yn, w_out, h, mods, norm_g, cond, *, o):
    tokens = h.shape[0]
    tm, nk = 512, 2
    gk = SSM_GROUPS // nk
    return pl.pallas_call(
        _odd_out_kernel,
        grid=(tokens // tm, nk),
        in_specs=[
            pl.BlockSpec((gk, tm, GROUP_WIDTH), lambda i, k: (k, i, 0)),
            pl.BlockSpec((None, gk * GROUP_WIDTH, D_MODEL), lambda i, k: (o, k, 0)),
            pl.BlockSpec((tm, D_MODEL), lambda i, k: (i, 0)),
            *cond.specs(tm),
        ],
        out_specs=pl.BlockSpec((tm, D_MODEL), lambda i, k: (i, 0)),
        out_shape=jax.ShapeDtypeStruct((tokens, D_MODEL), F32),
        compiler_params=_params(("parallel", "arbitrary")),
        name="odd_out",
    )(yn, w_out, h, mods, norm_g)


N_Z = D_INNER // GROUP_WIDTH
N_XBC = CONV_DIM // GROUP_WIDTH
X_CHUNKS = D_INNER // GROUP_WIDTH
BC_CHUNKS = SSM_GROUPS * D_STATE // GROUP_WIDTH
GROUPS_PER_CHUNK = GROUP_WIDTH // D_STATE


def _conv_silu_rows(p_ref, cw_ref, cb_ref, out_ref, seq):
    n = p_ref.shape[0]
    row = lax.broadcasted_iota(jnp.int32, (ROW_BLOCK, 1), 0)
    w_prev, w_cur, w_next, bias = cw_ref[0:1], cw_ref[1:2], cw_ref[2:3], cb_ref[...]
    for r in range(0, n, ROW_BLOCK):
        lo, hi = max(r - SUBLANES, 0), min(r + ROW_BLOCK + SUBLANES, n)
        blk = p_ref[lo:hi]
        cur = blk[r - lo:r - lo + ROW_BLOCK]
        prev = pltpu.roll(blk, 1, axis=0)[r - lo:r - lo + ROW_BLOCK]
        nxt = pltpu.roll(blk, hi - lo - 1, axis=0)[r - lo:r - lo + ROW_BLOCK]
        if r % seq == 0:
            prev = jnp.where(row == 0, 0.0, prev)
        if (r + ROW_BLOCK) % seq == 0:
            nxt = jnp.where(row == ROW_BLOCK - 1, 0.0, nxt)
        out_ref[r:r + ROW_BLOCK] = _silu(prev * w_prev + cur * w_cur + nxt * w_next + bias)


def _odd_in_kernel(h_ref, mods_ref, g_ref, w_ref, wdt_ref, cw_ref, cb_ref, dtb_ref,
                   z_ref, xbc_ref, dt_ref, u_ref, p_ref, *, seq):
    j = pl.program_id(1)
    last = pl.num_programs(1) - 1

    def project():
        return jnp.dot(u_ref[...], w_ref[...], preferred_element_type=F32)

    @pl.when(j == 0)
    def _():
        _modulate_rows(h_ref, g_ref[2:3], mods_ref[3:4], mods_ref[4:5], u_ref)
        raw = jnp.dot(u_ref[...], wdt_ref[...], preferred_element_type=F32) + dtb_ref[...]
        dt_ref[...] = jnp.maximum(raw, 0.0) + jnp.log1p(jnp.exp(-jnp.abs(raw)))

    @pl.when(j < N_Z)
    def _():
        z_ref[...] = project()

    @pl.when(j == N_Z)
    def _():
        p_ref[N_Z % 2] = project()

    for parity in range(2):
        @pl.when((j > N_Z) & (j < last) & (j % 2 == parity))
        def _(parity=parity):
            _conv_silu_rows(p_ref.at[1 - parity], cw_ref, cb_ref, xbc_ref, seq)
            p_ref[parity] = project()

    @pl.when(j == last)
    def _():
        _conv_silu_rows(p_ref.at[(N_Z + N_XBC - 1) % 2], cw_ref, cb_ref, xbc_ref, seq)


def _odd_in(h, mods, norm_g, w_in, conv_w, conv_b, dt_bias, cond, *, o):
    tokens = h.shape[0]
    seq = cond.seq
    tm, tn = 1024, GROUP_WIDTH
    assert tm % seq == 0 and tokens % tm == 0
    dt_block = (D_INNER + CONV_DIM) // LANES
    n_chunks = N_Z + N_XBC
    conv_idx = lambda j: jnp.maximum(j - 1 - N_Z, 0)
    return pl.pallas_call(
        functools.partial(_odd_in_kernel, seq=seq),
        grid=(tokens // tm, n_chunks + 1),
        in_specs=[
            pl.BlockSpec((tm, D_MODEL), lambda i, j: (i, 0)),
            *cond.specs(tm),
            pl.BlockSpec((None, D_MODEL, tn), lambda i, j: (o, 0, jnp.minimum(j, n_chunks - 1))),
            pl.BlockSpec((None, D_MODEL, 2 * SSM_HEADS), lambda i, j: (o, 0, dt_block)),
            pl.BlockSpec((None, 3, tn), lambda i, j: (o, 0, conv_idx(j))),
            pl.BlockSpec((None, 1, tn), lambda i, j: (o, 0, conv_idx(j))),
            pl.BlockSpec((None, 1, 2 * SSM_HEADS), lambda i, j: (o, 0, 0)),
        ],
        out_specs=[
            pl.BlockSpec((None, tm, tn), lambda i, j: (jnp.minimum(j, N_Z - 1), i, 0)),
            pl.BlockSpec((None, tm, tn), lambda i, j: (conv_idx(j), i, 0)),
            pl.BlockSpec((tm, 2 * SSM_HEADS), lambda i, j: (i, 0)),
        ],
        out_shape=[
            jax.ShapeDtypeStruct((N_Z, tokens, tn), F32),
            jax.ShapeDtypeStruct((N_XBC, tokens, tn), F32),
            jax.ShapeDtypeStruct((tokens, 2 * SSM_HEADS), F32),
        ],
        scratch_shapes=[pltpu.VMEM((tm, D_MODEL), BF16), pltpu.VMEM((2, tm, tn), F32)],
        compiler_params=_params(("parallel", "arbitrary")),
        name="odd_in",
    )(h, mods, norm_g, w_in, w_in, conv_w, conv_b, dt_bias)


def _split3(x):
    x1 = x.astype(BF16)
    r1 = x - x1.astype(F32)
    x2 = r1.astype(BF16)
    x3 = (r1 - x2.astype(F32)).astype(BF16)
    return x1, x2, x3


def _ssd_kernel(*refs, direction, has_h0, emit_state, combine):
    refs = list(refs)
    x_ref, b_ref, c_ref, dt_ref, alog_ref = refs[:5]
    del refs[:5]
    h0_ref = refs.pop(0) if has_h0 else None
    if combine:
        yf_ref, z_ref, dskip_ref, ng_ref = refs[:4]
        del refs[:4]
    other_state_ref = refs.pop(0) if (emit_state and combine) else None
    y_ref = refs.pop(0)
    state_out_ref = refs.pop(0) if emit_state else None
    ht_ref = refs.pop(0)
    yb_ref = refs.pop(0) if combine else None

    c = pl.program_id(1)
    q = SSM_CHUNK
    pairs = GROUP_WIDTH // LANES

    @pl.when(c == 0)
    def _():
        if has_h0:
            for g in range(SSM_GROUPS):
                for pr in range(pairs):
                    r0 = g * GROUP_WIDTH + pr * LANES
                    ht_ref[g, :, pr * LANES:(pr + 1) * LANES] = h0_ref[r0:r0 + LANES, :].T
        else:
            ht_ref[...] = jnp.zeros_like(ht_ref)

    ii = lax.broadcasted_iota(jnp.int32, (q, q), 0)
    jj = lax.broadcasted_iota(jnp.int32, (q, q), 1)
    causal = (jj <= ii) if direction == 0 else (jj >= ii)
    tri = jnp.where(causal, 1.0, 0.0).astype(BF16)
    low_lane = lax.broadcasted_iota(jnp.int32, (1, LANES), 1) < SSM_HEADDIM

    dt = dt_ref[...]
    dta = dt * (-jnp.exp(alog_ref[...]))
    a = sum(jnp.dot(tri, part, preferred_element_type=F32) for part in _split3(dta))
    a_end = a[q - 1:q] if direction == 0 else a[0:1]
    a_t = a.T
    dt_t = dt.T
    s_t = (dt * jnp.exp(a_end - a)).T
    end_scale = jnp.exp(a_end)

    for g in range(SSM_GROUPS):
        bc_lanes = slice((g % GROUPS_PER_CHUNK) * D_STATE, (g % GROUPS_PER_CHUNK + 1) * D_STATE)
        bg_t = b_ref[g // GROUPS_PER_CHUNK, :, bc_lanes].T
        cg32 = c_ref[g // GROUPS_PER_CHUNK, :, bc_lanes]
        cb = jnp.dot(cg32.astype(BF16), bg_t.astype(BF16), preferred_element_type=F32)
        for pr in range(pairs):
            lanes = slice(pr * LANES, (pr + 1) * LANES)
            xp = x_ref[g, :, lanes].astype(BF16)
            hp = ht_ref[g, :, lanes]
            hpb = hp.astype(BF16)
            ys, ds, scales = [], [], []
            for e in range(2):
                hl = direction * SSM_HEADS + g * HEADS_PER_GROUP + 2 * pr + e
                col = jnp.broadcast_to(a[:, hl:hl + 1], (q, q))
                decay = jnp.exp(jnp.where(causal, col - a_t[hl:hl + 1, :], -jnp.inf))
                w = (cb * decay * dt_t[hl:hl + 1, :]).astype(BF16)
                cdec = (cg32 * jnp.exp(col)).astype(BF16)
                ys.append(jnp.dot(w, xp, preferred_element_type=F32)
                          + jnp.dot(cdec, hpb, preferred_element_type=F32))
                bs = (bg_t * s_t[hl:hl + 1, :]).astype(BF16)
                ds.append(jnp.dot(bs, xp, preferred_element_type=F32))
                scales.append(jnp.broadcast_to(end_scale[:, hl:hl + 1], (1, LANES)))
            y_pair = jnp.where(low_lane, ys[0], ys[1])
            if combine:
                yb_ref[g, :, lanes] = y_pair
            else:
                y_ref[g, :, lanes] = y_pair
            ht_ref[g, :, lanes] = (hp * jnp.where(low_lane, scales[0], scales[1])
                                   + jnp.where(low_lane, ds[0], ds[1]))

    if combine:
        ssq = jnp.zeros((q, 1), F32)
        for g in range(SSM_GROUPS):
            xg = x_ref[g]
            y = (yf_ref[g] + dskip_ref[0, g] * xg) + (yb_ref[g] + dskip_ref[1, g] * xg)
            y = y * _silu(z_ref[g])
            yb_ref[g] = y
            ssq = ssq + jnp.sum(y * y, axis=-1, keepdims=True)
        inv = lax.rsqrt(ssq / D_INNER + EPS)
        for g in range(SSM_GROUPS):
            y_ref[g] = (yb_ref[g] * inv * ng_ref[g]).astype(BF16)

    if emit_state:
        @pl.when(c == pl.num_programs(1) - 1)
        def _():
            own = state_out_ref.at[direction] if combine else state_out_ref
            for g in range(SSM_GROUPS):
                for pr in range(pairs):
                    r0 = g * GROUP_WIDTH + pr * LANES
                    own[r0:r0 + LANES, :] = ht_ref[g, :, pr * LANES:(pr + 1) * LANES].T
            if combine:
                state_out_ref[1 - direction] = other_state_ref[...]


def _ssd(xbc, dt, a_log, h0, *, o, direction, batch, seq, emit_state, combine=None, other_state=None):
    tokens = xbc.shape[1]
    q = SSM_CHUNK
    nc = seq // q
    if direction == 0:
        chunk = lambda b, c: b * nc + c
    else:
        chunk = lambda b, c: b * nc + (nc - 1 - c)
    rows = lambda n, first: pl.BlockSpec((n, q, GROUP_WIDTH), lambda b, c: (first // n, chunk(b, c), 0))
    in_specs = [rows(X_CHUNKS, 0), rows(BC_CHUNKS, X_CHUNKS), rows(BC_CHUNKS, X_CHUNKS + BC_CHUNKS),
                pl.BlockSpec((q, 2 * SSM_HEADS), lambda b, c: (chunk(b, c), 0)),
                pl.BlockSpec((None, 1, 2 * SSM_HEADS), lambda b, c: (o, 0, 0))]
    args = [xbc, xbc, xbc, dt, a_log]
    if h0 is not None:
        in_specs.append(pl.BlockSpec((None, None, None, D_INNER, D_STATE), lambda b, c: (b, o, direction, 0, 0)))
        args.append(h0)
    if combine is not None:
        y_f, z, d_skip, norm_g = combine
        in_specs += [rows(X_CHUNKS, 0), rows(N_Z, 0),
                     pl.BlockSpec((None, 2, SSM_GROUPS, 1, GROUP_WIDTH), lambda b, c: (o, 0, 0, 0, 0)),
                     pl.BlockSpec((None, SSM_GROUPS, 1, GROUP_WIDTH), lambda b, c: (o, 0, 0, 0))]
        args += [y_f, z, d_skip, norm_g]
    out_specs = [rows(X_CHUNKS, 0)]
    out_shape = [jax.ShapeDtypeStruct((X_CHUNKS, tokens, GROUP_WIDTH), BF16 if combine is not None else F32)]
    if emit_state and combine is not None:
        in_specs.append(pl.BlockSpec((None, D_INNER, D_STATE), lambda b, c: (b, 0, 0)))
        args.append(other_state)
        out_specs.append(pl.BlockSpec((None, 2, D_INNER, D_STATE), lambda b, c: (b, 0, 0, 0)))
        out_shape.append(jax.ShapeDtypeStruct((batch, 2, D_INNER, D_STATE), F32))
    elif emit_state:
        out_specs.append(pl.BlockSpec((None, D_INNER, D_STATE), lambda b, c: (b, 0, 0)))
        out_shape.append(jax.ShapeDtypeStruct((batch, D_INNER, D_STATE), F32))
    scratch = [pltpu.VMEM((SSM_GROUPS, D_STATE, GROUP_WIDTH), F32)]
    if combine is not None:
        scratch.append(pltpu.VMEM((SSM_GROUPS, q, GROUP_WIDTH), F32))
    return pl.pallas_call(
        functools.partial(_ssd_kernel, direction=direction, has_h0=h0 is not None,
                          emit_state=emit_state, combine=combine is not None),
        grid=(batch, nc),
        in_specs=in_specs,
        out_specs=out_specs,
        out_shape=out_shape,
        scratch_shapes=scratch,
        compiler_params=_params(("parallel", "arbitrary")),
        name="ssd_fwd" if direction == 0 else "ssd_bwd",
    )(*args)


def _trunk(h, mods, w, *, row0, n_cond, batch, seq, cache_k=None, cache_v=None, state=None):
    context = cache_k is None
    outs = {}
    for l in range(DEPTH):
        cond = _Cond(l, row0, n_cond, seq)
        norm_g = w["norm_g"]
        h = _ffn(h, mods, norm_g, w["ffn_w_in"], w["ffn_w_out"], cond, half=0)
        if l % 2 == 0:
            e = l // 2
            pool_y, q, k, v = _even_in(h, mods, norm_g, w["mix_w_in"], w["pool_w"], w["pool_scale"],
                                       w["qk_norm_g"], cond, e=e, rope=not context)
            attn = _attention(q, k, v, cache_k, cache_v, e=e, batch=batch, seq=seq)
            h = _even_out(pool_y, attn, w["mix_w_out"], h, mods, norm_g, cond, e=e)
            outs["k"], outs["v"] = k, v
        else:
            o = l // 2
            z, xbc, dt = _odd_in(h, mods, norm_g, w["ssm_w_in"], w["ssm_conv_w"], w["ssm_conv_b"],
                                 w["ssm_dt_bias"], cond, o=o)
            scan = functools.partial(_ssd, xbc, dt, w["ssm_A_log"], state, o=o, batch=batch, seq=seq)
            fwd = scan(direction=0, emit_state=context)
            bwd = scan(direction=1, emit_state=context, other_state=fwd[1] if context else None,
                       combine=(fwd[0], z, w["ssm_D"], w["ssm_norm_g"]))
            if context:
                outs["ssm"] = bwd[1]
            h = _odd_out(bwd[0], w["ssm_w_out"], h, mods, norm_g, cond, o=o)
        h = _ffn(h, mods, norm_g, w["ffn_w_in"], w["ffn_w_out"], cond, half=1)
    return h, outs


def kernel(x_prompt, x_sample, cache_k, cache_v, state_ssm, c, c_ctx, ada_w, ada_b, norm_g, ffn_w_in,
           ffn_w_out, mix_w_in, pool_w, pool_scale, qk_norm_g, mix_w_out, ssm_w_in, ssm_conv_w,
           ssm_conv_b, ssm_dt_bias, ssm_A_log, ssm_D, ssm_norm_g, ssm_w_out):
    batch, seq, _ = x_prompt.shape
    dec_batch, dec_seq, _ = x_sample.shape
    n_even, n_odd = mix_w_in.shape[0], ssm_w_in.shape[0]
    past = cache_k.shape[2]

    cond = jnp.zeros((COND_ROWS, D_MODEL), F32).at[0].set(c_ctx).at[1:1 + dec_batch].set(c)
    mods = _ada_mods(cond, ada_w, ada_b).reshape(DEPTH, COND_ROWS, N_MOD, D_MODEL)

    w = dict(
        norm_g=norm_g,
        ffn_w_in=ffn_w_in.astype(BF16), ffn_w_out=ffn_w_out.astype(BF16),
        mix_w_in=mix_w_in.astype(BF16), pool_w=pool_w.astype(BF16),
        pool_scale=pool_scale.reshape(n_even, 1, POOL_WIDTH), qk_norm_g=qk_norm_g,
        mix_w_out=mix_w_out.astype(BF16),
        ssm_w_in=ssm_w_in.astype(BF16),
        ssm_conv_w=jnp.swapaxes(ssm_conv_w, 1, 2), ssm_conv_b=ssm_conv_b.reshape(n_odd, 1, CONV_DIM),
        ssm_dt_bias=ssm_dt_bias.reshape(n_odd, 1, 2 * SSM_HEADS),
        ssm_A_log=ssm_A_log.reshape(n_odd, 1, 2 * SSM_HEADS),
        ssm_D=jnp.repeat(ssm_D, SSM_HEADDIM, axis=-1).reshape(n_odd, 2, SSM_GROUPS, 1, GROUP_WIDTH),
        ssm_norm_g=ssm_norm_g.reshape(n_odd, SSM_GROUPS, 1, GROUP_WIDTH),
        ssm_w_out=ssm_w_out.astype(BF16),
    )

    y_prompt, ctx = _trunk(x_prompt.reshape(batch * seq, D_MODEL), mods, w, row0=0, n_cond=1,
                           batch=batch, seq=seq)
    y_sample, _ = _trunk(x_sample.reshape(dec_batch * dec_seq, D_MODEL), mods, w, row0=1, n_cond=dec_batch,
                         batch=dec_batch, seq=dec_seq,
                         cache_k=cache_k.reshape(dec_batch, n_even, past, KV_WIDTH),
                         cache_v=cache_v.reshape(dec_batch, n_even, past, KV_WIDTH),
                         state=state_ssm.reshape(dec_batch, n_odd, 2, D_INNER, D_STATE))

    new_k = ctx["k"].reshape(batch, n_even, seq, N_KV_HEADS, HEAD_DIM)
    new_v = ctx["v"].reshape(batch, n_even, seq, N_KV_HEADS, HEAD_DIM)
    new_ssm = ctx["ssm"].reshape(batch, n_odd, 2, SSM_HEADS, SSM_HEADDIM, D_STATE)
    return (y_prompt.reshape(batch, seq, D_MODEL), y_sample.reshape(dec_batch, dec_seq, D_MODEL),
            new_k, new_v, new_ssm)
```

```python
import functools

import jax
import jax.numpy as jnp
from jax import lax
from jax.experimental import pallas as pl
from jax.experimental.pallas import tpu as pltpu

F32 = jnp.float32
BF16 = jnp.bfloat16

D_MODEL = 2048
DEPTH = 2
GRID_W = 64
EPS = 1e-6
N_MOD = 9
N_NORM = 6
D_FF = 5632
POOL_WINDOWS = (2, 4, 8, 16)
POOL_WIDTH = 1024
POOL_GROUP_DIM = 256
HEAD_DIM = 128
N_KV_HEADS = 2
Q_PER_KV = 4
ATTN_WIDTH = 1024
KV_WIDTH = 256
MIX_IN = 2560
ROPE_THETA = 10000.0
D_INNER = 4096
SSM_HEADDIM = 64
SSM_HEADS = 64
SSM_GROUPS = 8
HEADS_PER_GROUP = 8
D_STATE = 128
SSM_CHUNK = 128
CONV_DIM = 6144
GROUP_WIDTH = HEADS_PER_GROUP * SSM_HEADDIM

LANES = 128
VMEM_LIMIT = 60 * 1024 * 1024
COND_ROWS = 16


def _params(sem):
    return pltpu.CompilerParams(dimension_semantics=sem, vmem_limit_bytes=VMEM_LIMIT)


def _rms(x, g):
    ms = jnp.mean(x * x, axis=-1, keepdims=True)
    return x * lax.rsqrt(ms + EPS) * g


def _silu(x):
    return x * jax.nn.sigmoid(x)


ROW_BLOCK = 16


def _modulate_rows(h_ref, g, shift, scale, u_ref):
    gain = g * (1 + scale)
    for r in range(0, h_ref.shape[0], ROW_BLOCK):
        x = h_ref[r:r + ROW_BLOCK]
        ms = jnp.mean(x * x, axis=-1, keepdims=True)
        u_ref[r:r + ROW_BLOCK] = (x * lax.rsqrt(ms + EPS) * gain + shift).astype(BF16)


def _residual_rows(h_ref, y_ref, g, gate, out_ref):
    gain = g * gate
    for r in range(0, h_ref.shape[0], ROW_BLOCK):
        y = y_ref[r:r + ROW_BLOCK]
        ms = jnp.mean(y * y, axis=-1, keepdims=True)
        out_ref[r:r + ROW_BLOCK] = h_ref[r:r + ROW_BLOCK] + y * lax.rsqrt(ms + EPS) * gain


class _Cond:
    def __init__(self, layer, row0, n_cond, seq):
        self.layer, self.row0, self.n_cond, self.seq = layer, row0, n_cond, seq

    def specs(self, tm):
        layer, row0 = self.layer, self.row0
        if self.n_cond == 1:
            row = lambda i: row0
        else:
            assert self.seq % tm == 0
            per_seq = self.seq // tm
            row = lambda i: row0 + i // per_seq
        return [pl.BlockSpec((None, None, N_MOD, D_MODEL), lambda i, *_: (layer, row(i), 0, 0)),
                pl.BlockSpec((None, N_NORM, D_MODEL), lambda *_: (layer, 0, 0))]


def _mods_kernel(cond_ref, w_ref, b_ref, out_ref):
    sc = _silu(cond_ref[...]).astype(BF16)
    w = w_ref[...].astype(BF16)
    out_ref[...] = jnp.dot(sc, w, preferred_element_type=F32) + b_ref[...]


def _ada_mods(cond, ada_w, ada_b):
    tn = 1024
    n_out = N_MOD * D_MODEL
    return pl.pallas_call(
        _mods_kernel,
        grid=(DEPTH, n_out // tn),
        in_specs=[
            pl.BlockSpec((COND_ROWS, D_MODEL), lambda l, n: (0, 0)),
            pl.BlockSpec((None, D_MODEL, tn), lambda l, n: (l, 0, n)),
            pl.BlockSpec((None, 1, tn), lambda l, n: (l, 0, n)),
        ],
        out_specs=pl.BlockSpec((None, COND_ROWS, tn), lambda l, n: (l, 0, n)),
        out_shape=jax.ShapeDtypeStruct((DEPTH, COND_ROWS, n_out), F32),
        compiler_params=_params(("parallel", "parallel")),
        name="ada_mods",
    )(cond, ada_w, ada_b.reshape(DEPTH, 1, n_out))


FFN_SUB = 256


def _ffn_kernel(h_ref, mods_ref, g_ref, wa_ref, wb_ref, wo_ref, out_ref, u_ref, ab_ref, hid_ref,
                *, mod0, g0):
    j = pl.program_id(1)
    n_sub = wo_ref.shape[0] // FFN_SUB

    def gate_rows(s):
        for r in range(0, ab_ref.shape[1], ROW_BLOCK):
            ab = ab_ref[s, r:r + ROW_BLOCK]
            hid_ref[s, r:r + ROW_BLOCK] = (_silu(ab[:, :FFN_SUB]) * ab[:, FFN_SUB:]).astype(BF16)

    def partial_out():
        u = u_ref[...]
        for s in range(n_sub):
            cols = slice(s * FFN_SUB, (s + 1) * FFN_SUB)
            w_ab = jnp.concatenate([wa_ref[:, cols], wb_ref[:, cols]], axis=1)
            ab_ref[s] = jnp.dot(u, w_ab, preferred_element_type=F32)
            gate_rows(s)
        y = jnp.dot(hid_ref[0], wo_ref[:FFN_SUB], preferred_element_type=F32)
        for s in range(1, n_sub):
            y = y + jnp.dot(hid_ref[s], wo_ref[s * FFN_SUB:(s + 1) * FFN_SUB], preferred_element_type=F32)
        return y

    @pl.when(j == 0)
    def _():
        _modulate_rows(h_ref, g_ref[g0:g0 + 1], mods_ref[mod0:mod0 + 1], mods_ref[mod0 + 1:mod0 + 2], u_ref)
        out_ref[...] = partial_out()

    @pl.when(j > 0)
    def _():
        out_ref[...] += partial_out()

    @pl.when(j == pl.num_programs(1) - 1)
    def _():
        _residual_rows(h_ref, out_ref, g_ref[g0 + 1:g0 + 2], 0.5 * mods_ref[mod0 + 2:mod0 + 3], out_ref)


def _ffn(h, mods, norm_g, w_in, w_out, cond, *, half):
    tokens = h.shape[0]
    tm, tf = 1024, 512
    nf = D_FF // tf
    layer = cond.layer
    return pl.pallas_call(
        functools.partial(_ffn_kernel, mod0=6 * half, g0=4 * half),
        grid=(tokens // tm, nf),
        in_specs=[
            pl.BlockSpec((tm, D_MODEL), lambda i, j: (i, 0)),
            *cond.specs(tm),
            pl.BlockSpec((None, None, D_MODEL, tf), lambda i, j: (layer, half, 0, j)),
            pl.BlockSpec((None, None, D_MODEL, tf), lambda i, j: (layer, half, 0, nf + j)),
            pl.BlockSpec((None, None, tf, D_MODEL), lambda i, j: (layer, half, j, 0)),
        ],
        out_specs=pl.BlockSpec((tm, D_MODEL), lambda i, j: (i, 0)),
        out_shape=jax.ShapeDtypeStruct((tokens, D_MODEL), F32),
        scratch_shapes=[pltpu.VMEM((tm, D_MODEL), BF16),
                        pltpu.VMEM((tf // FFN_SUB, tm, 2 * FFN_SUB), F32),
                        pltpu.VMEM((tf // FFN_SUB, tm, FFN_SUB), BF16)],
        compiler_params=_params(("parallel", "arbitrary")),
        name="ffn",
    )(h, mods, norm_g, w_in, w_in, w_out)


SUBLANES = 8


POOL_ROWS = 64


def _halo_block(p_ref, r, n_rows, cols, seq):
    zeros = jnp.zeros((SUBLANES, cols.stop - cols.start), F32)
    before = zeros if r % seq == 0 else p_ref[r - SUBLANES:r, cols]
    after = zeros if (r + n_rows) % seq == 0 else p_ref[r + n_rows:r + n_rows + SUBLANES, cols]
    return jnp.concatenate([before, p_ref[r:r + n_rows, cols], after], axis=0)


def _pool_delta_rows(p_ref, cols, window, seq, d_ref):
    half = window // 2
    assert half <= SUBLANES and seq % POOL_ROWS == 0
    n = p_ref.shape[0]
    rows = POOL_ROWS + 2 * SUBLANES
    row = lax.broadcasted_iota(jnp.int32, (POOL_ROWS, 1), 0)
    for r in range(0, n, POOL_ROWS):
        blk = _halo_block(p_ref, r, POOL_ROWS, cols, seq)
        fwd = blk
        bwd = blk
        m = 1
        while m < half:
            fwd = fwd + pltpu.roll(fwd, rows - m, axis=0)
            bwd = bwd + pltpu.roll(bwd, m, axis=0)
            m *= 2
        total = (fwd + pltpu.roll(bwd, 1, axis=0))[SUBLANES:SUBLANES + POOL_ROWS]
        pos = row + r % seq
        count = jnp.minimum(pos + half, seq) - jnp.maximum(pos - half, 0)
        d = total / count.astype(F32) - blk[SUBLANES:SUBLANES + POOL_ROWS]
        d_ref[r:r + POOL_ROWS, cols] = d.astype(BF16)


def _rope(x, cos, sin_lo, sin_hi):
    quarter = HEAD_DIM // 4
    return (x * cos + pltpu.roll(x, HEAD_DIM - quarter, axis=1) * sin_lo
            + pltpu.roll(x, quarter, axis=1) * sin_hi)


HEAD_ROWS_ROPE = 256


def _even_in_kernel(h_ref, mods_ref, g_ref, w_ref, pw_ref, ps_ref, qkg_ref, *rest, seq, rope):
    if rope:
        cos_ref, slo_ref, shi_ref, pool_ref, q_ref, k_ref, v_ref, u_ref, p_ref, d_ref = rest
    else:
        pool_ref, q_ref, k_ref, v_ref, u_ref, p_ref, d_ref = rest
    j = pl.program_id(1)
    tm = h_ref.shape[0]
    n_pool = POOL_WIDTH // p_ref.shape[2]
    n_q = ATTN_WIDTH // p_ref.shape[2]

    def project(chunk):
        p_ref[chunk % 2] = jnp.dot(u_ref[...], w_ref[...], preferred_element_type=F32)

    groups = p_ref.shape[2] // POOL_GROUP_DIM

    def pool_deltas(chunk):
        for gi in range(groups):
            cols = slice(gi * POOL_GROUP_DIM, (gi + 1) * POOL_GROUP_DIM)
            _pool_delta_rows(p_ref.at[chunk % 2], cols, POOL_WINDOWS[groups * chunk + gi], seq, d_ref)

    def pool_mix():
        for gi in range(groups):
            cols = slice(gi * POOL_GROUP_DIM, (gi + 1) * POOL_GROUP_DIM)
            y = jnp.dot(d_ref[:, cols], pw_ref[gi], preferred_element_type=F32)
            pool_ref[:, cols] = (y * ps_ref[:, cols]).astype(BF16)

    def heads_epilogue(chunk, n_heads, gain, dst_ref):
        src = p_ref.at[chunk % 2]
        step = HEAD_ROWS_ROPE if rope else ROW_BLOCK
        for r in range(0, tm, step):
            rows = slice(r, r + step)
            for hd in range(n_heads):
                cols = slice(hd * HEAD_DIM, (hd + 1) * HEAD_DIM)
                x = _rms(src[rows, cols], gain)
                if rope:
                    x = _rope(x, cos_ref[rows], slo_ref[rows], shi_ref[rows])
                dst_ref[rows, cols] = x.astype(dst_ref.dtype)

    def kv_epilogue(chunk):
        heads_epilogue(chunk, N_KV_HEADS, qkg_ref[1:2], k_ref)
        src = p_ref.at[chunk % 2]
        for r in range(0, tm, ROW_BLOCK):
            v_ref[r:r + ROW_BLOCK] = src[r:r + ROW_BLOCK, KV_WIDTH:]

    @pl.when(j == 0)
    def _():
        _modulate_rows(h_ref, g_ref[2:3], mods_ref[3:4], mods_ref[4:5], u_ref)
        project(0)

    for chunk in range(1, n_pool + n_q + 2):
        @pl.when(j == chunk)
        def _(chunk=chunk):
            done = chunk - 1
            if done < n_pool:
                pool_deltas(done)
            elif done < n_pool + n_q:
                heads_epilogue(done, Q_PER_KV, qkg_ref[0:1], q_ref)
            else:
                kv_epilogue(done)
            if chunk <= n_pool + n_q:
                project(chunk)
            if done < n_pool:
                pool_mix()


def _rope_tables(seq):
    pos = jnp.arange(seq)
    row = (pos // GRID_W).astype(F32)
    col = (pos % GRID_W).astype(F32)
    quarter = HEAD_DIM // 4
    inv_freq = ROPE_THETA ** (-jnp.arange(quarter, dtype=F32) / quarter)
    ang_r = row[:, None] * inv_freq[None]
    ang_c = col[:, None] * inv_freq[None]
    zero = jnp.zeros_like(ang_r)
    cos = jnp.concatenate([jnp.cos(ang_r), jnp.cos(ang_r), jnp.cos(ang_c), jnp.cos(ang_c)], axis=-1)
    sin_lo = jnp.concatenate([-jnp.sin(ang_r), zero, -jnp.sin(ang_c), zero], axis=-1)
    sin_hi = jnp.concatenate([zero, jnp.sin(ang_r), zero, jnp.sin(ang_c)], axis=-1)
    return cos, sin_lo, sin_hi


def _even_in(h, mods, norm_g, w_in, pool_w, pool_scale, qk_g, cond, *, e, rope):
    tokens = h.shape[0]
    seq = cond.seq
    tm, tn = 1024, 512
    assert tm % seq == 0 and tokens % tm == 0
    n_pool = POOL_WIDTH // tn
    n_q = ATTN_WIDTH // tn
    n_chunks = MIX_IN // tn
    pool_idx = lambda j: jnp.clip(j - 1, 0, n_pool - 1)
    q_idx = lambda j: jnp.clip(j - 1 - n_pool, 0, n_q - 1)
    in_specs = [
        pl.BlockSpec((tm, D_MODEL), lambda i, j: (i, 0)),
        *cond.specs(tm),
        pl.BlockSpec((None, D_MODEL, tn), lambda i, j: (e, 0, jnp.minimum(j, n_chunks - 1))),
        pl.BlockSpec((None, 2, POOL_GROUP_DIM, POOL_GROUP_DIM), lambda i, j: (e, pool_idx(j), 0, 0)),
        pl.BlockSpec((None, 1, tn), lambda i, j: (e, 0, pool_idx(j))),
        pl.BlockSpec((None, 2, HEAD_DIM), lambda i, j: (e, 0, 0)),
    ]
    args = [h, mods, norm_g, w_in, pool_w, pool_scale, qk_g]
    if rope:
        assert tm == seq
        in_specs += [pl.BlockSpec((tm, HEAD_DIM), lambda i, j: (0, 0))] * 3
        args += list(_rope_tables(seq))
    return pl.pallas_call(
        functools.partial(_even_in_kernel, seq=seq, rope=rope),
        grid=(tokens // tm, n_chunks + 1),
        in_specs=in_specs,
        out_specs=[
            pl.BlockSpec((tm, tn), lambda i, j: (i, pool_idx(j))),
            pl.BlockSpec((tm, tn), lambda i, j: (i, q_idx(j))),
            pl.BlockSpec((tm, KV_WIDTH), lambda i, j: (i, 0)),
            pl.BlockSpec((tm, KV_WIDTH), lambda i, j: (i, 0)),
        ],
        out_shape=[
            jax.ShapeDtypeStruct((tokens, POOL_WIDTH), BF16),
            jax.ShapeDtypeStruct((tokens, ATTN_WIDTH), BF16),
            jax.ShapeDtypeStruct((tokens, KV_WIDTH), F32),
            jax.ShapeDtypeStruct((tokens, KV_WIDTH), F32),
        ],
        scratch_shapes=[pltpu.VMEM((tm, D_MODEL), BF16), pltpu.VMEM((2, tm, tn), F32),
                        pltpu.VMEM((tm, tn), BF16)],
        compiler_params=_params(("parallel", "arbitrary")),
        name="even_in",
    )(*args)


def _attn_kernel(q_ref, k_ref, v_ref, *rest, cached):
    if cached:
        ck_ref, cv_ref, o_ref = rest
    else:
        (o_ref,) = rest
    scale = HEAD_DIM ** -0.5
    nt = (((1,), (1,)), ((), ()))
    kb = k_ref[...].astype(BF16)
    vb = v_ref[...].astype(BF16)
    if cached:
        ckb = ck_ref[...].astype(BF16)
        cvb = cv_ref[...].astype(BF16)
    for hd in range(Q_PER_KV):
        sl = slice(hd * HEAD_DIM, (hd + 1) * HEAD_DIM)
        q = q_ref[:, sl]
        s = lax.dot_general(q, kb, nt, preferred_element_type=F32) * scale
        m = jnp.max(s, axis=-1, keepdims=True)
        if cached:
            sc = lax.dot_general(q, ckb, nt, preferred_element_type=F32) * scale
            m = jnp.maximum(m, jnp.max(sc, axis=-1, keepdims=True))
        e = jnp.exp(s - m)
        denom = jnp.sum(e, axis=-1, keepdims=True)
        o = jnp.dot(e.astype(BF16), vb, preferred_element_type=F32)
        if cached:
            ec = jnp.exp(sc - m)
            denom = denom + jnp.sum(ec, axis=-1, keepdims=True)
            o = o + jnp.dot(ec.astype(BF16), cvb, preferred_element_type=F32)
        o_ref[:, sl] = (o / denom).astype(BF16)


def _attention(q, k, v, cache_k, cache_v, *, e, batch, seq):
    tokens = q.shape[0]
    tq = 256
    nq = seq // tq
    gw = Q_PER_KV * HEAD_DIM
    cached = cache_k is not None
    in_specs = [
        pl.BlockSpec((tq, gw), lambda b, g, i: (b * nq + i, g)),
        pl.BlockSpec((seq, HEAD_DIM), lambda b, g, i: (b, g)),
        pl.BlockSpec((seq, HEAD_DIM), lambda b, g, i: (b, g)),
    ]
    args = [q, k, v]
    if cached:
        past = cache_k.shape[2]
        in_specs += [pl.BlockSpec((None, None, past, HEAD_DIM), lambda b, g, i: (b, e, 0, g))] * 2
        args += [cache_k, cache_v]
    return pl.pallas_call(
        functools.partial(_attn_kernel, cached=cached),
        grid=(batch, N_KV_HEADS, nq),
        in_specs=in_specs,
        out_specs=pl.BlockSpec((tq, gw), lambda b, g, i: (b * nq + i, g)),
        out_shape=jax.ShapeDtypeStruct((tokens, ATTN_WIDTH), BF16),
        compiler_params=_params(("parallel", "parallel", "parallel")),
        name="attention",
    )(*args)


def _even_out_kernel(pool_ref, attn_ref, w_ref, h_ref, mods_ref, g_ref, out_ref):
    y = jnp.dot(pool_ref[...], w_ref[:POOL_WIDTH], preferred_element_type=F32)
    y = y + jnp.dot(attn_ref[...], w_ref[POOL_WIDTH:], preferred_element_type=F32)
    out_ref[...] = y
    _residual_rows(h_ref, out_ref, g_ref[3:4], mods_ref[5:6], out_ref)


def _even_out(pool_y, attn, w_out, h, mods, norm_g, cond, *, e):
    tokens = h.shape[0]
    tm = 512
    return pl.pallas_call(
        _even_out_kernel,
        grid=(tokens // tm,),
        in_specs=[
            pl.BlockSpec((tm, POOL_WIDTH), lambda i: (i, 0)),
            pl.BlockSpec((tm, ATTN_WIDTH), lambda i: (i, 0)),
            pl.BlockSpec((None, POOL_WIDTH + ATTN_WIDTH, D_MODEL), lambda i: (e, 0, 0)),
            pl.BlockSpec((tm, D_MODEL), lambda i: (i, 0)),
            *cond.specs(tm),
        ],
        out_specs=pl.BlockSpec((tm, D_MODEL), lambda i: (i, 0)),
        out_shape=jax.ShapeDtypeStruct((tokens, D_MODEL), F32),
        compiler_params=_params(("parallel",)),
        name="even_out",
    )(pool_y, attn, w_out, h, mods, norm_g)


def _odd_out_kernel(x_ref, w_ref, h_ref, mods_ref, g_ref, out_ref):
    k = pl.program_id(1)
    y = jnp.dot(x_ref[0], w_ref[:GROUP_WIDTH], preferred_element_type=F32)
    for g in range(1, x_ref.shape[0]):
        y = y + jnp.dot(x_ref[g], w_ref[g * GROUP_WIDTH:(g + 1) * GROUP_WIDTH], preferred_element_type=F32)

    @pl.when(k == 0)
    def _():
        out_ref[...] = y

    @pl.when(k == pl.num_programs(1) - 1)
    def _():
        out_ref[...] += y
        _residual_rows(h_ref, out_ref, g_ref[3:4], mods_ref[5:6], out_ref)


def _odd_out(yn, w_out, h, mods, norm_g, cond, *, o):
    tokens = h.shape[0]
    tm, nk = 512, 2
    gk = SSM_GROUPS // nk
    return pl.pallas_call(
        _odd_out_kernel,
        grid=(tokens // tm, nk),
        in_specs=[
            pl.BlockSpec((gk, tm, GROUP_WIDTH), lambda i, k: (k, i, 0)),
            pl.BlockSpec((None, gk * GROUP_WIDTH, D_MODEL), lambda i, k: (o, k, 0)),
            pl.BlockSpec((tm, D_MODEL), lambda i, k: (i, 0)),
            *cond.specs(tm),
        ],
        out_specs=pl.BlockSpec((tm, D_MODEL), lambda i, k: (i, 0)),
        out_shape=jax.ShapeDtypeStruct((tokens, D_MODEL), F32),
        compiler_params=_params(("parallel", "arbitrary")),
        name="odd_out",
    )(yn, w_out, h, mods, norm_g)


N_Z = D_INNER // GROUP_WIDTH
N_XBC = CONV_DIM // GROUP_WIDTH
X_CHUNKS = D_INNER // GROUP_WIDTH
BC_CHUNKS = SSM_GROUPS * D_STATE // GROUP_WIDTH
GROUPS_PER_CHUNK = GROUP_WIDTH // D_STATE


def _conv_silu_rows(p_ref, cw_ref, cb_ref, out_ref, seq):
    n = p_ref.shape[0]
    row = lax.broadcasted_iota(jnp.int32, (ROW_BLOCK, 1), 0)
    w_prev, w_cur, w_next, bias = cw_ref[0:1], cw_ref[1:2], cw_ref[2:3], cb_ref[...]
    for r in range(0, n, ROW_BLOCK):
        lo, hi = max(r - SUBLANES, 0), min(r + ROW_BLOCK + SUBLANES, n)
        blk = p_ref[lo:hi]
        cur = blk[r - lo:r - lo + ROW_BLOCK]
        prev = pltpu.roll(blk, 1, axis=0)[r - lo:r - lo + ROW_BLOCK]
        nxt = pltpu.roll(blk, hi - lo - 1, axis=0)[r - lo:r - lo + ROW_BLOCK]
        if r % seq == 0:
            prev = jnp.where(row == 0, 0.0, prev)
        if (r + ROW_BLOCK) % seq == 0:
            nxt = jnp.where(row == ROW_BLOCK - 1, 0.0, nxt)
        out_ref[r:r + ROW_BLOCK] = _silu(prev * w_prev + cur * w_cur + nxt * w_next + bias)


def _odd_in_kernel(h_ref, mods_ref, g_ref, w_ref, wdt_ref, cw_ref, cb_ref, dtb_ref,
                   z_ref, xbc_ref, dt_ref, u_ref, p_ref, *, seq):
    j = pl.program_id(1)

    def project():
        return jnp.dot(u_ref[...], w_ref[...], preferred_element_type=F32)

    @pl.when(j == 0)
    def _():
        _modulate_rows(h_ref, g_ref[2:3], mods_ref[3:4], mods_ref[4:5], u_ref)
        raw = jnp.dot(u_ref[...], wdt_ref[...], preferred_element_type=F32) + dtb_ref[...]
        dt_ref[...] = jnp.maximum(raw, 0.0) + jnp.log1p(jnp.exp(-jnp.abs(raw)))

    @pl.when(j < N_Z)
    def _():
        z_ref[...] = project()

    @pl.when(j >= N_Z)
    def _():
        p_ref[...] = project()
        _conv_silu_rows(p_ref, cw_ref, cb_ref, xbc_ref, seq)


def _odd_in(h, mods, norm_g, w_in, conv_w, conv_b, dt_bias, cond, *, o):
    tokens = h.shape[0]
    seq = cond.seq
    tm, tn = 1024, GROUP_WIDTH
    assert tm % seq == 0 and tokens % tm == 0
    dt_block = (D_INNER + CONV_DIM) // LANES
    conv_idx = lambda j: jnp.maximum(j - N_Z, 0)
    return pl.pallas_call(
        functools.partial(_odd_in_kernel, seq=seq),
        grid=(tokens // tm, N_Z + N_XBC),
        in_specs=[
            pl.BlockSpec((tm, D_MODEL), lambda i, j: (i, 0)),
            *cond.specs(tm),
            pl.BlockSpec((None, D_MODEL, tn), lambda i, j: (o, 0, j)),
            pl.BlockSpec((None, D_MODEL, 2 * SSM_HEADS), lambda i, j: (o, 0, dt_block)),
            pl.BlockSpec((None, 3, tn), lambda i, j: (o, 0, conv_idx(j))),
            pl.BlockSpec((None, 1, tn), lambda i, j: (o, 0, conv_idx(j))),
            pl.BlockSpec((None, 1, 2 * SSM_HEADS), lambda i, j: (o, 0, 0)),
        ],
        out_specs=[
            pl.BlockSpec((None, tm, tn), lambda i, j: (jnp.minimum(j, N_Z - 1), i, 0)),
            pl.BlockSpec((None, tm, tn), lambda i, j: (conv_idx(j), i, 0)),
            pl.BlockSpec((tm, 2 * SSM_HEADS), lambda i, j: (i, 0)),
        ],
        out_shape=[
            jax.ShapeDtypeStruct((N_Z, tokens, tn), F32),
            jax.ShapeDtypeStruct((N_XBC, tokens, tn), F32),
            jax.ShapeDtypeStruct((tokens, 2 * SSM_HEADS), F32),
        ],
        scratch_shapes=[pltpu.VMEM((tm, D_MODEL), BF16), pltpu.VMEM((tm, tn), F32)],
        compiler_params=_params(("parallel", "arbitrary")),
        name="odd_in",
    )(h, mods, norm_g, w_in, w_in, conv_w, conv_b, dt_bias)


def _split3(x):
    x1 = x.astype(BF16)
    r1 = x - x1.astype(F32)
    x2 = r1.astype(BF16)
    x3 = (r1 - x2.astype(F32)).astype(BF16)
    return x1, x2, x3


def _ssd_kernel(*refs, direction, has_h0, emit_state, combine):
    refs = list(refs)
    x_ref, b_ref, c_ref, dt_ref, alog_ref = refs[:5]
    del refs[:5]
    h0_ref = refs.pop(0) if has_h0 else None
    if combine:
        yf_ref, z_ref, dskip_ref, ng_ref = refs[:4]
        del refs[:4]
    other_state_ref = refs.pop(0) if (emit_state and combine) else None
    y_ref = refs.pop(0)
    state_out_ref = refs.pop(0) if emit_state else None
    ht_ref = refs.pop(0)
    yb_ref = refs.pop(0) if combine else None

    c = pl.program_id(1)
    q = SSM_CHUNK
    pairs = GROUP_WIDTH // LANES

    @pl.when(c == 0)
    def _():
        if has_h0:
            for g in range(SSM_GROUPS):
                for pr in range(pairs):
                    r0 = g * GROUP_WIDTH + pr * LANES
                    ht_ref[g, :, pr * LANES:(pr + 1) * LANES] = h0_ref[r0:r0 + LANES, :].T
        else:
            ht_ref[...] = jnp.zeros_like(ht_ref)

    ii = lax.broadcasted_iota(jnp.int32, (q, q), 0)
    jj = lax.broadcasted_iota(jnp.int32, (q, q), 1)
    causal = (jj <= ii) if direction == 0 else (jj >= ii)
    tri = jnp.where(causal, 1.0, 0.0).astype(BF16)
    low_lane = lax.broadcasted_iota(jnp.int32, (1, LANES), 1) < SSM_HEADDIM

    dt = dt_ref[...]
    dta = dt * (-jnp.exp(alog_ref[...]))
    a = sum(jnp.dot(tri, part, preferred_element_type=F32) for part in _split3(dta))
    a_end = a[q - 1:q] if direction == 0 else a[0:1]
    a_t = a.T
    dt_t = dt.T
    s_t = (dt * jnp.exp(a_end - a)).T
    end_scale = jnp.exp(a_end)

    for g in range(SSM_GROUPS):
        bc_lanes = slice((g % GROUPS_PER_CHUNK) * D_STATE, (g % GROUPS_PER_CHUNK + 1) * D_STATE)
        bg_t = b_ref[g // GROUPS_PER_CHUNK, :, bc_lanes].T
        cg32 = c_ref[g // GROUPS_PER_CHUNK, :, bc_lanes]
        cb = jnp.dot(cg32.astype(BF16), bg_t.astype(BF16), preferred_element_type=F32)
        for pr in range(pairs):
            lanes = slice(pr * LANES, (pr + 1) * LANES)
            xp = x_ref[g, :, lanes].astype(BF16)
            hp = ht_ref[g, :, lanes]
            hpb = hp.astype(BF16)
            ys, ds, scales = [], [], []
            for e in range(2):
                hl = direction * SSM_HEADS + g * HEADS_PER_GROUP + 2 * pr + e
                col = jnp.broadcast_to(a[:, hl:hl + 1], (q, q))
                decay = jnp.exp(jnp.where(causal, col - a_t[hl:hl + 1, :], -jnp.inf))
                w = (cb * decay * dt_t[hl:hl + 1, :]).astype(BF16)
                cdec = (cg32 * jnp.exp(col)).astype(BF16)
                ys.append(jnp.dot(w, xp, preferred_element_type=F32)
                          + jnp.dot(cdec, hpb, preferred_element_type=F32))
                bs = (bg_t * s_t[hl:hl + 1, :]).astype(BF16)
                ds.append(jnp.dot(bs, xp, preferred_element_type=F32))
                scales.append(jnp.broadcast_to(end_scale[:, hl:hl + 1], (1, LANES)))
            y_pair = jnp.where(low_lane, ys[0], ys[1])
            if combine:
                yb_ref[g, :, lanes] = y_pair
            else:
                y_ref[g, :, lanes] = y_pair
            ht_ref[g, :, lanes] = (hp * jnp.where(low_lane, scales[0], scales[1])
                                   + jnp.where(low_lane, ds[0], ds[1]))

    if combine:
        ssq = jnp.zeros((q, 1), F32)
        for g in range(SSM_GROUPS):
            xg = x_ref[g]
            y = (yf_ref[g] + dskip_ref[0, g] * xg) + (yb_ref[g] + dskip_ref[1, g] * xg)
            y = y * _silu(z_ref[g])
            yb_ref[g] = y
            ssq = ssq + jnp.sum(y * y, axis=-1, keepdims=True)
        inv = lax.rsqrt(ssq / D_INNER + EPS)
        for g in range(SSM_GROUPS):
            y_ref[g] = (yb_ref[g] * inv * ng_ref[g]).astype(BF16)

    if emit_state:
        @pl.when(c == pl.num_programs(1) - 1)
        def _():
            own = state_out_ref.at[direction] if combine else state_out_ref
            for g in range(SSM_GROUPS):
                for pr in range(pairs):
                    r0 = g * GROUP_WIDTH + pr * LANES
                    own[r0:r0 + LANES, :] = ht_ref[g, :, pr * LANES:(pr + 1) * LANES].T
            if combine:
                state_out_ref[1 - direction] = other_state_ref[...]


def _ssd(xbc, dt, a_log, h0, *, o, direction, batch, seq, emit_state, combine=None, other_state=None):
    tokens = xbc.shape[1]
    q = SSM_CHUNK
    nc = seq // q
    if direction == 0:
        chunk = lambda b, c: b * nc + c
    else:
        chunk = lambda b, c: b * nc + (nc - 1 - c)
    rows = lambda n, first: pl.BlockSpec((n, q, GROUP_WIDTH), lambda b, c: (first // n, chunk(b, c), 0))
    in_specs = [rows(X_CHUNKS, 0), rows(BC_CHUNKS, X_CHUNKS), rows(BC_CHUNKS, X_CHUNKS + BC_CHUNKS),
                pl.BlockSpec((q, 2 * SSM_HEADS), lambda b, c: (chunk(b, c), 0)),
                pl.BlockSpec((None, 1, 2 * SSM_HEADS), lambda b, c: (o, 0, 0))]
    args = [xbc, xbc, xbc, dt, a_log]
    if h0 is not None:
        in_specs.append(pl.BlockSpec((None, None, None, D_INNER, D_STATE), lambda b, c: (b, o, direction, 0, 0)))
        args.append(h0)
    if combine is not None:
        y_f, z, d_skip, norm_g = combine
        in_specs += [rows(X_CHUNKS, 0), rows(N_Z, 0),
                     pl.BlockSpec((None, 2, SSM_GROUPS, 1, GROUP_WIDTH), lambda b, c: (o, 0, 0, 0, 0)),
                     pl.BlockSpec((None, SSM_GROUPS, 1, GROUP_WIDTH), lambda b, c: (o, 0, 0, 0))]
        args += [y_f, z, d_skip, norm_g]
    out_specs = [rows(X_CHUNKS, 0)]
    out_shape = [jax.ShapeDtypeStruct((X_CHUNKS, tokens, GROUP_WIDTH), BF16 if combine is not None else F32)]
    if emit_state and combine is not None:
        in_specs.append(pl.BlockSpec((None, D_INNER, D_STATE), lambda b, c: (b, 0, 0)))
        args.append(other_state)
        out_specs.append(pl.BlockSpec((None, 2, D_INNER, D_STATE), lambda b, c: (b, 0, 0, 0)))
        out_shape.append(jax.ShapeDtypeStruct((batch, 2, D_INNER, D_STATE), F32))
    elif emit_state:
        out_specs.append(pl.BlockSpec((None, D_INNER, D_STATE), lambda b, c: (b, 0, 0)))
        out_shape.append(jax.ShapeDtypeStruct((batch, D_INNER, D_STATE), F32))
    scratch = [pltpu.VMEM((SSM_GROUPS, D_STATE, GROUP_WIDTH), F32)]
    if combine is not None:
        scratch.append(pltpu.VMEM((SSM_GROUPS, q, GROUP_WIDTH), F32))
    return pl.pallas_call(
        functools.partial(_ssd_kernel, direction=direction, has_h0=h0 is not None,
                          emit_state=emit_state, combine=combine is not None),
        grid=(batch, nc),
        in_specs=in_specs,
        out_specs=out_specs,
        out_shape=out_shape,
        scratch_shapes=scratch,
        compiler_params=_params(("parallel", "arbitrary")),
        name="ssd_fwd" if direction == 0 else "ssd_bwd",
    )(*args)


def _trunk(h, mods, w, *, row0, n_cond, batch, seq, cache_k=None, cache_v=None, state=None):
    context = cache_k is None
    outs = {}
    for l in range(DEPTH):
        cond = _Cond(l, row0, n_cond, seq)
        norm_g = w["norm_g"]
        h = _ffn(h, mods, norm_g, w["ffn_w_in"], w["ffn_w_out"], cond, half=0)
        if l % 2 == 0:
            e = l // 2
            pool_y, q, k, v = _even_in(h, mods, norm_g, w["mix_w_in"], w["pool_w"], w["pool_scale"],
                                       w["qk_norm_g"], cond, e=e, rope=not context)
            attn = _attention(q, k, v, cache_k, cache_v, e=e, batch=batch, seq=seq)
            h = _even_out(pool_y, attn, w["mix_w_out"], h, mods, norm_g, cond, e=e)
            outs["k"], outs["v"] = k, v
        else:
            o = l // 2
            z, xbc, dt = _odd_in(h, mods, norm_g, w["ssm_w_in"], w["ssm_conv_w"], w["ssm_conv_b"],
                                 w["ssm_dt_bias"], cond, o=o)
            scan = functools.partial(_ssd, xbc, dt, w["ssm_A_log"], state, o=o, batch=batch, seq=seq)
            fwd = scan(direction=0, emit_state=context)
            bwd = scan(direction=1, emit_state=context, other_state=fwd[1] if context else None,
                       combine=(fwd[0], z, w["ssm_D"], w["ssm_norm_g"]))
            if context:
                outs["ssm"] = bwd[1]
            h = _odd_out(bwd[0], w["ssm_w_out"], h, mods, norm_g, cond, o=o)
        h = _ffn(h, mods, norm_g, w["ffn_w_in"], w["ffn_w_out"], cond, half=1)
    return h, outs


def kernel(x_prompt, x_sample, cache_k, cache_v, state_ssm, c, c_ctx, ada_w, ada_b, norm_g, ffn_w_in,
           ffn_w_out, mix_w_in, pool_w, pool_scale, qk_norm_g, mix_w_out, ssm_w_in, ssm_conv_w,
           ssm_conv_b, ssm_dt_bias, ssm_A_log, ssm_D, ssm_norm_g, ssm_w_out):
    batch, seq, _ = x_prompt.shape
    dec_batch, dec_seq, _ = x_sample.shape
    n_even, n_odd = mix_w_in.shape[0], ssm_w_in.shape[0]
    past = cache_k.shape[2]

    cond = jnp.zeros((COND_ROWS, D_MODEL), F32).at[0].set(c_ctx).at[1:1 + dec_batch].set(c)
    mods = _ada_mods(cond, ada_w, ada_b).reshape(DEPTH, COND_ROWS, N_MOD, D_MODEL)

    w = dict(
        norm_g=norm_g,
        ffn_w_in=ffn_w_in.astype(BF16), ffn_w_out=ffn_w_out.astype(BF16),
        mix_w_in=mix_w_in.astype(BF16), pool_w=pool_w.astype(BF16),
        pool_scale=pool_scale.reshape(n_even, 1, POOL_WIDTH), qk_norm_g=qk_norm_g,
        mix_w_out=mix_w_out.astype(BF16),
        ssm_w_in=ssm_w_in.astype(BF16),
        ssm_conv_w=jnp.swapaxes(ssm_conv_w, 1, 2), ssm_conv_b=ssm_conv_b.reshape(n_odd, 1, CONV_DIM),
        ssm_dt_bias=ssm_dt_bias.reshape(n_odd, 1, 2 * SSM_HEADS),
        ssm_A_log=ssm_A_log.reshape(n_odd, 1, 2 * SSM_HEADS),
        ssm_D=jnp.repeat(ssm_D, SSM_HEADDIM, axis=-1).reshape(n_odd, 2, SSM_GROUPS, 1, GROUP_WIDTH),
        ssm_norm_g=ssm_norm_g.reshape(n_odd, SSM_GROUPS, 1, GROUP_WIDTH),
        ssm_w_out=ssm_w_out.astype(BF16),
    )

    y_prompt, ctx = _trunk(x_prompt.reshape(batch * seq, D_MODEL), mods, w, row0=0, n_cond=1,
                           batch=batch, seq=seq)
    y_sample, _ = _trunk(x_sample.reshape(dec_batch * dec_seq, D_MODEL), mods, w, row0=1, n_cond=dec_batch,
                         batch=dec_batch, seq=dec_seq,
                         cache_k=cache_k.reshape(dec_batch, n_even, past, KV_WIDTH),
                         cache_v=cache_v.reshape(dec_batch, n_even, past, KV_WIDTH),
                         state=state_ssm.reshape(dec_batch, n_odd, 2, D_INNER, D_STATE))

    new_k = ctx["k"].reshape(batch, n_even, seq, N_KV_HEADS, HEAD_DIM)
    new_v = ctx["v"].reshape(batch, n_even, seq, N_KV_HEADS, HEAD_DIM)
    new_ssm = ctx["ssm"].reshape(batch, n_odd, 2, SSM_HEADS, SSM_HEADDIM, D_STATE)
    return (y_prompt.reshape(batch, seq, D_MODEL), y_sample.reshape(dec_batch, dec_seq, D_MODEL),
            new_k, new_v, new_ssm)
```

```python
import functools

import jax
import jax.numpy as jnp
from jax import lax
from jax.experimental import pallas as pl
from jax.experimental.pallas import tpu as pltpu

F32 = jnp.float32
BF16 = jnp.bfloat16

D_MODEL = 2048
DEPTH = 2
GRID_W = 64
EPS = 1e-6
N_MOD = 9
N_NORM = 6
D_FF = 5632
POOL_WINDOWS = (2, 4, 8, 16)
POOL_WIDTH = 1024
POOL_GROUP_DIM = 256
HEAD_DIM = 128
N_KV_HEADS = 2
Q_PER_KV = 4
ATTN_WIDTH = 1024
KV_WIDTH = 256
MIX_IN = 2560
ROPE_THETA = 10000.0
D_INNER = 4096
SSM_HEADDIM = 64
SSM_HEADS = 64
SSM_GROUPS = 8
HEADS_PER_GROUP = 8
D_STATE = 128
SSM_CHUNK = 128
CONV_DIM = 6144
GROUP_WIDTH = HEADS_PER_GROUP * SSM_HEADDIM

LANES = 128
VMEM_LIMIT = 60 * 1024 * 1024
COND_ROWS = 16


def _params(sem):
    return pltpu.CompilerParams(dimension_semantics=sem, vmem_limit_bytes=VMEM_LIMIT)


def _rms(x, g):
    ms = jnp.mean(x * x, axis=-1, keepdims=True)
    return x * lax.rsqrt(ms + EPS) * g


def _silu(x):
    return x * jax.nn.sigmoid(x)


ROW_BLOCK = 16


def _modulate_rows(h_ref, g, shift, scale, u_ref):
    gain = g * (1 + scale)
    for r in range(0, h_ref.shape[0], ROW_BLOCK):
        x = h_ref[r:r + ROW_BLOCK]
        ms = jnp.mean(x * x, axis=-1, keepdims=True)
        u_ref[r:r + ROW_BLOCK] = (x * lax.rsqrt(ms + EPS) * gain + shift).astype(BF16)


def _residual_rows(h_ref, y_ref, g, gate, out_ref):
    gain = g * gate
    for r in range(0, h_ref.shape[0], ROW_BLOCK):
        y = y_ref[r:r + ROW_BLOCK]
        ms = jnp.mean(y * y, axis=-1, keepdims=True)
        out_ref[r:r + ROW_BLOCK] = h_ref[r:r + ROW_BLOCK] + y * lax.rsqrt(ms + EPS) * gain


class _Cond:
    def __init__(self, layer, row0, n_cond, seq):
        self.layer, self.row0, self.n_cond, self.seq = layer, row0, n_cond, seq

    def specs(self, tm):
        layer, row0 = self.layer, self.row0
        if self.n_cond == 1:
            row = lambda i: row0
        else:
            assert self.seq % tm == 0
            per_seq = self.seq // tm
            row = lambda i: row0 + i // per_seq
        return [pl.BlockSpec((None, None, N_MOD, D_MODEL), lambda i, *_: (layer, row(i), 0, 0)),
                pl.BlockSpec((None, N_NORM, D_MODEL), lambda *_: (layer, 0, 0))]


def _mods_kernel(cond_ref, w_ref, b_ref, out_ref):
    sc = _silu(cond_ref[...]).astype(BF16)
    w = w_ref[...].astype(BF16)
    out_ref[...] = jnp.dot(sc, w, preferred_element_type=F32) + b_ref[...]


def _ada_mods(cond, ada_w, ada_b):
    tn = 1024
    n_out = N_MOD * D_MODEL
    return pl.pallas_call(
        _mods_kernel,
        grid=(DEPTH, n_out // tn),
        in_specs=[
            pl.BlockSpec((COND_ROWS, D_MODEL), lambda l, n: (0, 0)),
            pl.BlockSpec((None, D_MODEL, tn), lambda l, n: (l, 0, n)),
            pl.BlockSpec((None, 1, tn), lambda l, n: (l, 0, n)),
        ],
        out_specs=pl.BlockSpec((None, COND_ROWS, tn), lambda l, n: (l, 0, n)),
        out_shape=jax.ShapeDtypeStruct((DEPTH, COND_ROWS, n_out), F32),
        compiler_params=_params(("parallel", "parallel")),
        name="ada_mods",
    )(cond, ada_w, ada_b.reshape(DEPTH, 1, n_out))


FFN_SUB = 256


def _ffn_kernel(h_ref, mods_ref, g_ref, wa_ref, wb_ref, wo_ref, out_ref, u_ref, ab_ref, hid_ref,
                *, mod0, g0):
    j = pl.program_id(1)
    n_sub = wo_ref.shape[0] // FFN_SUB

    def gate_rows(s):
        for r in range(0, ab_ref.shape[1], ROW_BLOCK):
            ab = ab_ref[s, r:r + ROW_BLOCK]
            hid_ref[s, r:r + ROW_BLOCK] = (_silu(ab[:, :FFN_SUB]) * ab[:, FFN_SUB:]).astype(BF16)

    def partial_out():
        u = u_ref[...]
        for s in range(n_sub):
            cols = slice(s * FFN_SUB, (s + 1) * FFN_SUB)
            w_ab = jnp.concatenate([wa_ref[:, cols], wb_ref[:, cols]], axis=1)
            ab_ref[s] = jnp.dot(u, w_ab, preferred_element_type=F32)
            gate_rows(s)
        y = jnp.dot(hid_ref[0], wo_ref[:FFN_SUB], preferred_element_type=F32)
        for s in range(1, n_sub):
            y = y + jnp.dot(hid_ref[s], wo_ref[s * FFN_SUB:(s + 1) * FFN_SUB], preferred_element_type=F32)
        return y

    @pl.when(j == 0)
    def _():
        _modulate_rows(h_ref, g_ref[g0:g0 + 1], mods_ref[mod0:mod0 + 1], mods_ref[mod0 + 1:mod0 + 2], u_ref)
        out_ref[...] = partial_out()

    @pl.when(j > 0)
    def _():
        out_ref[...] += partial_out()

    @pl.when(j == pl.num_programs(1) - 1)
    def _():
        _residual_rows(h_ref, out_ref, g_ref[g0 + 1:g0 + 2], 0.5 * mods_ref[mod0 + 2:mod0 + 3], out_ref)


def _ffn(h, mods, norm_g, w_in, w_out, cond, *, half):
    tokens = h.shape[0]
    tm, tf = 1024, 512
    nf = D_FF // tf
    layer = cond.layer
    return pl.pallas_call(
        functools.partial(_ffn_kernel, mod0=6 * half, g0=4 * half),
        grid=(tokens // tm, nf),
        in_specs=[
            pl.BlockSpec((tm, D_MODEL), lambda i, j: (i, 0)),
            *cond.specs(tm),
            pl.BlockSpec((None, None, D_MODEL, tf), lambda i, j: (layer, half, 0, j)),
            pl.BlockSpec((None, None, D_MODEL, tf), lambda i, j: (layer, half, 0, nf + j)),
            pl.BlockSpec((None, None, tf, D_MODEL), lambda i, j: (layer, half, j, 0)),
        ],
        out_specs=pl.BlockSpec((tm, D_MODEL), lambda i, j: (i, 0)),
        out_shape=jax.ShapeDtypeStruct((tokens, D_MODEL), F32),
        scratch_shapes=[pltpu.VMEM((tm, D_MODEL), BF16),
                        pltpu.VMEM((tf // FFN_SUB, tm, 2 * FFN_SUB), F32),
                        pltpu.VMEM((tf // FFN_SUB, tm, FFN_SUB), BF16)],
        compiler_params=_params(("parallel", "arbitrary")),
        name="ffn",
    )(h, mods, norm_g, w_in, w_in, w_out)


SUBLANES = 8


POOL_ROWS = 64


def _halo_block(p_ref, r, n_rows, cols, seq):
    zeros = jnp.zeros((SUBLANES, cols.stop - cols.start), F32)
    before = zeros if r % seq == 0 else p_ref[r - SUBLANES:r, cols]
    after = zeros if (r + n_rows) % seq == 0 else p_ref[r + n_rows:r + n_rows + SUBLANES, cols]
    return jnp.concatenate([before, p_ref[r:r + n_rows, cols], after], axis=0)


def _pool_delta_rows(p_ref, cols, window, seq, d_ref):
    half = window // 2
    assert half <= SUBLANES and seq % POOL_ROWS == 0
    n = p_ref.shape[0]
    rows = POOL_ROWS + 2 * SUBLANES
    row = lax.broadcasted_iota(jnp.int32, (POOL_ROWS, 1), 0)
    for r in range(0, n, POOL_ROWS):
        blk = _halo_block(p_ref, r, POOL_ROWS, cols, seq)
        fwd = blk
        bwd = blk
        m = 1
        while m < half:
            fwd = fwd + pltpu.roll(fwd, rows - m, axis=0)
            bwd = bwd + pltpu.roll(bwd, m, axis=0)
            m *= 2
        total = (fwd + pltpu.roll(bwd, 1, axis=0))[SUBLANES:SUBLANES + POOL_ROWS]
        pos = row + r % seq
        count = jnp.minimum(pos + half, seq) - jnp.maximum(pos - half, 0)
        d = total / count.astype(F32) - blk[SUBLANES:SUBLANES + POOL_ROWS]
        d_ref[r:r + POOL_ROWS, cols] = d.astype(BF16)


def _rope(x, cos, sin_lo, sin_hi):
    quarter = HEAD_DIM // 4
    return (x * cos + pltpu.roll(x, HEAD_DIM - quarter, axis=1) * sin_lo
            + pltpu.roll(x, quarter, axis=1) * sin_hi)


HEAD_ROWS_ROPE = 256


def _even_in_kernel(h_ref, mods_ref, g_ref, w_ref, pw_ref, ps_ref, qkg_ref, *rest, seq, rope):
    if rope:
        cos_ref, slo_ref, shi_ref, pool_ref, q_ref, k_ref, v_ref, u_ref, p_ref, d_ref = rest
    else:
        pool_ref, q_ref, k_ref, v_ref, u_ref, p_ref, d_ref = rest
    j = pl.program_id(1)
    tm = h_ref.shape[0]
    n_pool = POOL_WIDTH // p_ref.shape[2]
    n_q = ATTN_WIDTH // p_ref.shape[2]

    def project(chunk):
        p_ref[chunk % 2] = jnp.dot(u_ref[...], w_ref[...], preferred_element_type=F32)

    groups = p_ref.shape[2] // POOL_GROUP_DIM

    def pool_deltas(chunk):
        for gi in range(groups):
            cols = slice(gi * POOL_GROUP_DIM, (gi + 1) * POOL_GROUP_DIM)
            _pool_delta_rows(p_ref.at[chunk % 2], cols, POOL_WINDOWS[groups * chunk + gi], seq, d_ref)

    def pool_mix():
        for gi in range(groups):
            cols = slice(gi * POOL_GROUP_DIM, (gi + 1) * POOL_GROUP_DIM)
            y = jnp.dot(d_ref[:, cols], pw_ref[gi], preferred_element_type=F32)
            pool_ref[:, cols] = (y * ps_ref[:, cols]).astype(BF16)

    def heads_epilogue(chunk, n_heads, gain, dst_ref):
        src = p_ref.at[chunk % 2]
        step = HEAD_ROWS_ROPE if rope else ROW_BLOCK
        for r in range(0, tm, step):
            rows = slice(r, r + step)
            for hd in range(n_heads):
                cols = slice(hd * HEAD_DIM, (hd + 1) * HEAD_DIM)
                x = _rms(src[rows, cols], gain)
                if rope:
                    x = _rope(x, cos_ref[rows], slo_ref[rows], shi_ref[rows])
                dst_ref[rows, cols] = x.astype(dst_ref.dtype)

    def kv_epilogue(chunk):
        heads_epilogue(chunk, N_KV_HEADS, qkg_ref[1:2], k_ref)
        src = p_ref.at[chunk % 2]
        for r in range(0, tm, ROW_BLOCK):
            v_ref[r:r + ROW_BLOCK] = src[r:r + ROW_BLOCK, KV_WIDTH:]

    @pl.when(j == 0)
    def _():
        _modulate_rows(h_ref, g_ref[2:3], mods_ref[3:4], mods_ref[4:5], u_ref)
        project(0)

    for chunk in range(1, n_pool + n_q + 2):
        @pl.when(j == chunk)
        def _(chunk=chunk):
            done = chunk - 1
            if done < n_pool:
                pool_deltas(done)
            elif done < n_pool + n_q:
                heads_epilogue(done, Q_PER_KV, qkg_ref[0:1], q_ref)
            else:
                kv_epilogue(done)
            if chunk <= n_pool + n_q:
                project(chunk)
            if done < n_pool:
                pool_mix()


def _rope_tables(seq):
    pos = jnp.arange(seq)
    row = (pos // GRID_W).astype(F32)
    col = (pos % GRID_W).astype(F32)
    quarter = HEAD_DIM // 4
    inv_freq = ROPE_THETA ** (-jnp.arange(quarter, dtype=F32) / quarter)
    ang_r = row[:, None] * inv_freq[None]
    ang_c = col[:, None] * inv_freq[None]
    zero = jnp.zeros_like(ang_r)
    cos = jnp.concatenate([jnp.cos(ang_r), jnp.cos(ang_r), jnp.cos(ang_c), jnp.cos(ang_c)], axis=-1)
    sin_lo = jnp.concatenate([-jnp.sin(ang_r), zero, -jnp.sin(ang_c), zero], axis=-1)
    sin_hi = jnp.concatenate([zero, jnp.sin(ang_r), zero, jnp.sin(ang_c)], axis=-1)
    return cos, sin_lo, sin_hi


def _even_in(h, mods, norm_g, w_in, pool_w, pool_scale, qk_g, cond, *, e, rope):
    tokens = h.shape[0]
    seq = cond.seq
    tm, tn = 1024, 512
    assert tm % seq == 0 and tokens % tm == 0
    n_pool = POOL_WIDTH // tn
    n_q = ATTN_WIDTH // tn
    n_chunks = MIX_IN // tn
    pool_idx = lambda j: jnp.clip(j - 1, 0, n_pool - 1)
    q_idx = lambda j: jnp.clip(j - 1 - n_pool, 0, n_q - 1)
    in_specs = [
        pl.BlockSpec((tm, D_MODEL), lambda i, j: (i, 0)),
        *cond.specs(tm),
        pl.BlockSpec((None, D_MODEL, tn), lambda i, j: (e, 0, jnp.minimum(j, n_chunks - 1))),
        pl.BlockSpec((None, 2, POOL_GROUP_DIM, POOL_GROUP_DIM), lambda i, j: (e, pool_idx(j), 0, 0)),
        pl.BlockSpec((None, 1, tn), lambda i, j: (e, 0, pool_idx(j))),
        pl.BlockSpec((None, 2, HEAD_DIM), lambda i, j: (e, 0, 0)),
    ]
    args = [h, mods, norm_g, w_in, pool_w, pool_scale, qk_g]
    if rope:
        assert tm == seq
        in_specs += [pl.BlockSpec((tm, HEAD_DIM), lambda i, j: (0, 0))] * 3
        args += list(_rope_tables(seq))
    return pl.pallas_call(
        functools.partial(_even_in_kernel, seq=seq, rope=rope),
        grid=(tokens // tm, n_chunks + 1),
        in_specs=in_specs,
        out_specs=[
            pl.BlockSpec((tm, tn), lambda i, j: (i, pool_idx(j))),
            pl.BlockSpec((tm, tn), lambda i, j: (i, q_idx(j))),
            pl.BlockSpec((tm, KV_WIDTH), lambda i, j: (i, 0)),
            pl.BlockSpec((tm, KV_WIDTH), lambda i, j: (i, 0)),
        ],
        out_shape=[
            jax.ShapeDtypeStruct((tokens, POOL_WIDTH), BF16),
            jax.ShapeDtypeStruct((tokens, ATTN_WIDTH), BF16),
            jax.ShapeDtypeStruct((tokens, KV_WIDTH), F32),
            jax.ShapeDtypeStruct((tokens, KV_WIDTH), F32),
        ],
        scratch_shapes=[pltpu.VMEM((tm, D_MODEL), BF16), pltpu.VMEM((2, tm, tn), F32),
                        pltpu.VMEM((tm, tn), BF16)],
        compiler_params=_params(("parallel", "arbitrary")),
        name="even_in",
    )(*args)


def _attn_kernel(q_ref, k_ref, v_ref, *rest, cached):
    if cached:
        ck_ref, cv_ref, o_ref = rest
    else:
        (o_ref,) = rest
    scale = HEAD_DIM ** -0.5
    nt = (((1,), (1,)), ((), ()))
    kb = k_ref[...].astype(BF16)
    vb = v_ref[...].astype(BF16)
    if cached:
        ckb = ck_ref[...].astype(BF16)
        cvb = cv_ref[...].astype(BF16)
    for hd in range(Q_PER_KV):
        sl = slice(hd * HEAD_DIM, (hd + 1) * HEAD_DIM)
        q = q_ref[:, sl]
        s = lax.dot_general(q, kb, nt, preferred_element_type=F32) * scale
        m = jnp.max(s, axis=-1, keepdims=True)
        if cached:
            sc = lax.dot_general(q, ckb, nt, preferred_element_type=F32) * scale
            m = jnp.maximum(m, jnp.max(sc, axis=-1, keepdims=True))
        e = jnp.exp(s - m)
        denom = jnp.sum(e, axis=-1, keepdims=True)
        o = jnp.dot(e.astype(BF16), vb, preferred_element_type=F32)
        if cached:
            ec = jnp.exp(sc - m)
            denom = denom + jnp.sum(ec, axis=-1, keepdims=True)
            o = o + jnp.dot(ec.astype(BF16), cvb, preferred_element_type=F32)
        o_ref[:, sl] = (o / denom).astype(BF16)


def _attention(q, k, v, cache_k, cache_v, *, e, batch, seq):
    tokens = q.shape[0]
    tq = 256
    nq = seq // tq
    gw = Q_PER_KV * HEAD_DIM
    cached = cache_k is not None
    in_specs = [
        pl.BlockSpec((tq, gw), lambda b, g, i: (b * nq + i, g)),
        pl.BlockSpec((seq, HEAD_DIM), lambda b, g, i: (b, g)),
        pl.BlockSpec((seq, HEAD_DIM), lambda b, g, i: (b, g)),
    ]
    args = [q, k, v]
    if cached:
        past = cache_k.shape[2]
        in_specs += [pl.BlockSpec((None, None, past, HEAD_DIM), lambda b, g, i: (b, e, 0, g))] * 2
        args += [cache_k, cache_v]
    return pl.pallas_call(
        functools.partial(_attn_kernel, cached=cached),
        grid=(batch, N_KV_HEADS, nq),
        in_specs=in_specs,
        out_specs=pl.BlockSpec((tq, gw), lambda b, g, i: (b * nq + i, g)),
        out_shape=jax.ShapeDtypeStruct((tokens, ATTN_WIDTH), BF16),
        compiler_params=_params(("parallel", "parallel", "parallel")),
        name="attention",
    )(*args)


def _even_out_kernel(pool_ref, attn_ref, w_ref, h_ref, mods_ref, g_ref, out_ref):
    y = jnp.dot(pool_ref[...], w_ref[:POOL_WIDTH], preferred_element_type=F32)
    y = y + jnp.dot(attn_ref[...], w_ref[POOL_WIDTH:], preferred_element_type=F32)
    out_ref[...] = y
    _residual_rows(h_ref, out_ref, g_ref[3:4], mods_ref[5:6], out_ref)


def _even_out(pool_y, attn, w_out, h, mods, norm_g, cond, *, e):
    tokens = h.shape[0]
    tm = 512
    return pl.pallas_call(
        _even_out_kernel,
        grid=(tokens // tm,),
        in_specs=[
            pl.BlockSpec((tm, POOL_WIDTH), lambda i: (i, 0)),
            pl.BlockSpec((tm, ATTN_WIDTH), lambda i: (i, 0)),
            pl.BlockSpec((None, POOL_WIDTH + ATTN_WIDTH, D_MODEL), lambda i: (e, 0, 0)),
            pl.BlockSpec((tm, D_MODEL), lambda i: (i, 0)),
            *cond.specs(tm),
        ],
        out_specs=pl.BlockSpec((tm, D_MODEL), lambda i: (i, 0)),
        out_shape=jax.ShapeDtypeStruct((tokens, D_MODEL), F32),
        compiler_params=_params(("parallel",)),
        name="even_out",
    )(pool_y, attn, w_out, h, mods, norm_g)


def _odd_out_kernel(x_ref, w_ref, h_ref, mods_ref, g_ref, out_ref):
    k = pl.program_id(1)
    y = jnp.dot(x_ref[0], w_ref[:GROUP_WIDTH], preferred_element_type=F32)
    for g in range(1, x_ref.shape[0]):
        y = y + jnp.dot(x_ref[g], w_ref[g * GROUP_WIDTH:(g + 1) * GROUP_WIDTH], preferred_element_type=F32)

    @pl.when(k == 0)
    def _():
        out_ref[...] = y

    @pl.when(k == pl.num_programs(1) - 1)
    def _():
        out_ref[...] += y
        _residual_rows(h_ref, out_ref, g_ref[3:4], mods_ref[5:6], out_ref)


def _odd_out(yn, w_out, h, mods, norm_g, cond, *, o):
    tokens = h.shape[0]
    tm, nk = 512, 2
    gk = SSM_GROUPS // nk
    return pl.pallas_call(
        _odd_out_kernel,
        grid=(tokens // tm, nk),
        in_specs=[
            pl.BlockSpec((gk, tm, GROUP_WIDTH), lambda i, k: (k, i, 0)),
            pl.BlockSpec((None, gk * GROUP_WIDTH, D_MODEL), lambda i, k: (o, k, 0)),
            pl.BlockSpec((tm, D_MODEL), lambda i, k: (i, 0)),
            *cond.specs(tm),
        ],
        out_specs=pl.BlockSpec((tm, D_MODEL), lambda i, k: (i, 0)),
        out_shape=jax.ShapeDtypeStruct((tokens, D_MODEL), F32),
        compiler_params=_params(("parallel", "arbitrary")),
        name="odd_out",
    )(yn, w_out, h, mods, norm_g)


N_Z = D_INNER // GROUP_WIDTH
N_XBC = CONV_DIM // GROUP_WIDTH
X_CHUNKS = D_INNER // GROUP_WIDTH
BC_CHUNKS = SSM_GROUPS * D_STATE // GROUP_WIDTH
GROUPS_PER_CHUNK = GROUP_WIDTH // D_STATE
ODD_IN_CHUNKS = 2


def _conv_silu_rows(p_ref, cols, cw_ref, cb_ref, out_ref, seq):
    n = p_ref.shape[0]
    row = lax.broadcasted_iota(jnp.int32, (ROW_BLOCK, 1), 0)
    w_prev, w_cur, w_next, bias = cw_ref[0:1, cols], cw_ref[1:2, cols], cw_ref[2:3, cols], cb_ref[:, cols]
    for r in range(0, n, ROW_BLOCK):
        lo, hi = max(r - SUBLANES, 0), min(r + ROW_BLOCK + SUBLANES, n)
        blk = p_ref[lo:hi, cols]
        cur = blk[r - lo:r - lo + ROW_BLOCK]
        prev = pltpu.roll(blk, 1, axis=0)[r - lo:r - lo + ROW_BLOCK]
        nxt = pltpu.roll(blk, hi - lo - 1, axis=0)[r - lo:r - lo + ROW_BLOCK]
        if r % seq == 0:
            prev = jnp.where(row == 0, 0.0, prev)
        if (r + ROW_BLOCK) % seq == 0:
            nxt = jnp.where(row == ROW_BLOCK - 1, 0.0, nxt)
        out_ref[r:r + ROW_BLOCK] = _silu(prev * w_prev + cur * w_cur + nxt * w_next + bias)


def _odd_in_kernel(h_ref, mods_ref, g_ref, w_ref, wdt_ref, cw_ref, cb_ref, dtb_ref,
                   z_ref, xbc_ref, dt_ref, u_ref, p_ref, *, seq):
    j = pl.program_id(1)

    def project():
        return jnp.dot(u_ref[...], w_ref[...], preferred_element_type=F32)

    def chunk_cols(c):
        return slice(c * GROUP_WIDTH, (c + 1) * GROUP_WIDTH)

    @pl.when(j == 0)
    def _():
        _modulate_rows(h_ref, g_ref[2:3], mods_ref[3:4], mods_ref[4:5], u_ref)
        raw = jnp.dot(u_ref[...], wdt_ref[...], preferred_element_type=F32) + dtb_ref[...]
        dt_ref[...] = jnp.maximum(raw, 0.0) + jnp.log1p(jnp.exp(-jnp.abs(raw)))

    @pl.when(j < N_Z // ODD_IN_CHUNKS)
    def _():
        p = project()
        for c in range(ODD_IN_CHUNKS):
            z_ref[c] = p[:, chunk_cols(c)]

    @pl.when(j >= N_Z // ODD_IN_CHUNKS)
    def _():
        p_ref[...] = project()
        for c in range(ODD_IN_CHUNKS):
            _conv_silu_rows(p_ref, chunk_cols(c), cw_ref, cb_ref, xbc_ref.at[c], seq)


def _odd_in(h, mods, norm_g, w_in, conv_w, conv_b, dt_bias, cond, *, o):
    tokens = h.shape[0]
    seq = cond.seq
    tm, tn = 1024, ODD_IN_CHUNKS * GROUP_WIDTH
    assert tm % seq == 0 and tokens % tm == 0
    assert N_Z % ODD_IN_CHUNKS == 0 and N_XBC % ODD_IN_CHUNKS == 0
    nz = N_Z // ODD_IN_CHUNKS
    dt_block = (D_INNER + CONV_DIM) // LANES
    conv_idx = lambda j: jnp.maximum(j - nz, 0)
    return pl.pallas_call(
        functools.partial(_odd_in_kernel, seq=seq),
        grid=(tokens // tm, (N_Z + N_XBC) // ODD_IN_CHUNKS),
        in_specs=[
            pl.BlockSpec((tm, D_MODEL), lambda i, j: (i, 0)),
            *cond.specs(tm),
            pl.BlockSpec((None, D_MODEL, tn), lambda i, j: (o, 0, j)),
            pl.BlockSpec((None, D_MODEL, 2 * SSM_HEADS), lambda i, j: (o, 0, dt_block)),
            pl.BlockSpec((None, 3, tn), lambda i, j: (o, 0, conv_idx(j))),
            pl.BlockSpec((None, 1, tn), lambda i, j: (o, 0, conv_idx(j))),
            pl.BlockSpec((None, 1, 2 * SSM_HEADS), lambda i, j: (o, 0, 0)),
        ],
        out_specs=[
            pl.BlockSpec((ODD_IN_CHUNKS, tm, GROUP_WIDTH), lambda i, j: (jnp.minimum(j, nz - 1), i, 0)),
            pl.BlockSpec((ODD_IN_CHUNKS, tm, GROUP_WIDTH), lambda i, j: (conv_idx(j), i, 0)),
            pl.BlockSpec((tm, 2 * SSM_HEADS), lambda i, j: (i, 0)),
        ],
        out_shape=[
            jax.ShapeDtypeStruct((N_Z, tokens, GROUP_WIDTH), F32),
            jax.ShapeDtypeStruct((N_XBC, tokens, GROUP_WIDTH), F32),
            jax.ShapeDtypeStruct((tokens, 2 * SSM_HEADS), F32),
        ],
        scratch_shapes=[pltpu.VMEM((tm, D_MODEL), BF16), pltpu.VMEM((tm, tn), F32)],
        compiler_params=_params(("parallel", "arbitrary")),
        name="odd_in",
    )(h, mods, norm_g, w_in, w_in, conv_w, conv_b, dt_bias)


def _split3(x):
    x1 = x.astype(BF16)
    r1 = x - x1.astype(F32)
    x2 = r1.astype(BF16)
    x3 = (r1 - x2.astype(F32)).astype(BF16)
    return x1, x2, x3


def _ssd_kernel(*refs, direction, has_h0, emit_state, combine):
    refs = list(refs)
    x_ref, b_ref, c_ref, dt_ref, alog_ref = refs[:5]
    del refs[:5]
    h0_ref = refs.pop(0) if has_h0 else None
    if combine:
        yf_ref, z_ref, dskip_ref, ng_ref = refs[:4]
        del refs[:4]
    other_state_ref = refs.pop(0) if (emit_state and combine) else None
    y_ref = refs.pop(0)
    state_out_ref = refs.pop(0) if emit_state else None
    ht_ref = refs.pop(0)
    ops_ref = refs.pop(0)
    yb_ref = refs.pop(0) if combine else None

    c = pl.program_id(1)
    q = SSM_CHUNK
    pairs = GROUP_WIDTH // LANES

    @pl.when(c == 0)
    def _():
        if has_h0:
            for g in range(SSM_GROUPS):
                for pr in range(pairs):
                    r0 = g * GROUP_WIDTH + pr * LANES
                    ht_ref[g, :, pr * LANES:(pr + 1) * LANES] = h0_ref[r0:r0 + LANES, :].T
        else:
            ht_ref[...] = jnp.zeros_like(ht_ref)

    ii = lax.broadcasted_iota(jnp.int32, (q, q), 0)
    jj = lax.broadcasted_iota(jnp.int32, (q, q), 1)
    causal = (jj <= ii) if direction == 0 else (jj >= ii)
    tri = jnp.where(causal, 1.0, 0.0).astype(BF16)
    low_lane = lax.broadcasted_iota(jnp.int32, (1, LANES), 1) < SSM_HEADDIM

    dt = dt_ref[...]
    dta = dt * (-jnp.exp(alog_ref[...]))
    a = sum(jnp.dot(tri, part, preferred_element_type=F32) for part in _split3(dta))
    a_end = a[q - 1:q] if direction == 0 else a[0:1]
    a_t = a.T
    dt_t = dt.T
    s_t = (dt * jnp.exp(a_end - a)).T
    end_scale = jnp.exp(a_end)

    group_vals = {}

    def group(g):
        if g not in group_vals:
            group_vals.clear()
            bc_lanes = slice((g % GROUPS_PER_CHUNK) * D_STATE, (g % GROUPS_PER_CHUNK + 1) * D_STATE)
            bg_t = b_ref[g // GROUPS_PER_CHUNK, :, bc_lanes].T
            cg32 = c_ref[g // GROUPS_PER_CHUNK, :, bc_lanes]
            cb = jnp.dot(cg32.astype(BF16), bg_t.astype(BF16), preferred_element_type=F32)
            group_vals[g] = (bg_t, cg32, cb)
        return group_vals[g]

    def prepare(n):
        g, pr = divmod(n, pairs)
        bg_t, cg32, cb = group(g)
        for e in range(2):
            hl = direction * SSM_HEADS + g * HEADS_PER_GROUP + 2 * pr + e
            col = jnp.broadcast_to(a[:, hl:hl + 1], (q, q))
            decay = jnp.exp(jnp.where(causal, col - a_t[hl:hl + 1, :], -jnp.inf))
            ops_ref[n % 2, 3 * e] = (cb * decay * dt_t[hl:hl + 1, :]).astype(BF16)
            ops_ref[n % 2, 3 * e + 1] = (cg32 * jnp.exp(col)).astype(BF16)
            ops_ref[n % 2, 3 * e + 2] = (bg_t * s_t[hl:hl + 1, :]).astype(BF16)

    def consume(n):
        g, pr = divmod(n, pairs)
        lanes = slice(pr * LANES, (pr + 1) * LANES)
        xp = x_ref[g, :, lanes].astype(BF16)
        hp = ht_ref[g, :, lanes]
        hpb = hp.astype(BF16)
        ys, ds, scales = [], [], []
        for e in range(2):
            hl = direction * SSM_HEADS + g * HEADS_PER_GROUP + 2 * pr + e
            ys.append(jnp.dot(ops_ref[n % 2, 3 * e], xp, preferred_element_type=F32)
                      + jnp.dot(ops_ref[n % 2, 3 * e + 1], hpb, preferred_element_type=F32))
            ds.append(jnp.dot(ops_ref[n % 2, 3 * e + 2], xp, preferred_element_type=F32))
            scales.append(jnp.broadcast_to(end_scale[:, hl:hl + 1], (1, LANES)))
        y_pair = jnp.where(low_lane, ys[0], ys[1])
        if combine:
            yb_ref[g, :, lanes] = y_pair
        else:
            y_ref[g, :, lanes] = y_pair
        ht_ref[g, :, lanes] = (hp * jnp.where(low_lane, scales[0], scales[1])
                               + jnp.where(low_lane, ds[0], ds[1]))

    n_pairs = SSM_GROUPS * pairs
    prepare(0)
    for n in range(n_pairs):
        if n + 1 < n_pairs:
            prepare(n + 1)
        consume(n)

    if combine:
        ssq = jnp.zeros((q, 1), F32)
        for g in range(SSM_GROUPS):
            xg = x_ref[g]
            y = (yf_ref[g] + dskip_ref[0, g] * xg) + (yb_ref[g] + dskip_ref[1, g] * xg)
            y = y * _silu(z_ref[g])
            yb_ref[g] = y
            ssq = ssq + jnp.sum(y * y, axis=-1, keepdims=True)
        inv = lax.rsqrt(ssq / D_INNER + EPS)
        for g in range(SSM_GROUPS):
            y_ref[g] = (yb_ref[g] * inv * ng_ref[g]).astype(BF16)

    if emit_state:
        @pl.when(c == pl.num_programs(1) - 1)
        def _():
            own = state_out_ref.at[direction] if combine else state_out_ref
            for g in range(SSM_GROUPS):
                for pr in range(pairs):
                    r0 = g * GROUP_WIDTH + pr * LANES
                    own[r0:r0 + LANES, :] = ht_ref[g, :, pr * LANES:(pr + 1) * LANES].T
            if combine:
                state_out_ref[1 - direction] = other_state_ref[...]


def _ssd(xbc, dt, a_log, h0, *, o, direction, batch, seq, emit_state, combine=None, other_state=None):
    tokens = xbc.shape[1]
    q = SSM_CHUNK
    nc = seq // q
    if direction == 0:
        chunk = lambda b, c: b * nc + c
    else:
        chunk = lambda b, c: b * nc + (nc - 1 - c)
    rows = lambda n, first: pl.BlockSpec((n, q, GROUP_WIDTH), lambda b, c: (first // n, chunk(b, c), 0))
    in_specs = [rows(X_CHUNKS, 0), rows(BC_CHUNKS, X_CHUNKS), rows(BC_CHUNKS, X_CHUNKS + BC_CHUNKS),
                pl.BlockSpec((q, 2 * SSM_HEADS), lambda b, c: (chunk(b, c), 0)),
                pl.BlockSpec((None, 1, 2 * SSM_HEADS), lambda b, c: (o, 0, 0))]
    args = [xbc, xbc, xbc, dt, a_log]
    if h0 is not None:
        in_specs.append(pl.BlockSpec((None, None, None, D_INNER, D_STATE), lambda b, c: (b, o, direction, 0, 0)))
        args.append(h0)
    if combine is not None:
        y_f, z, d_skip, norm_g = combine
        in_specs += [rows(X_CHUNKS, 0), rows(N_Z, 0),
                     pl.BlockSpec((None, 2, SSM_GROUPS, 1, GROUP_WIDTH), lambda b, c: (o, 0, 0, 0, 0)),
                     pl.BlockSpec((None, SSM_GROUPS, 1, GROUP_WIDTH), lambda b, c: (o, 0, 0, 0))]
        args += [y_f, z, d_skip, norm_g]
    out_specs = [rows(X_CHUNKS, 0)]
    out_shape = [jax.ShapeDtypeStruct((X_CHUNKS, tokens, GROUP_WIDTH), BF16 if combine is not None else F32)]
    if emit_state and combine is not None:
        in_specs.append(pl.BlockSpec((None, D_INNER, D_STATE), lambda b, c: (b, 0, 0)))
        args.append(other_state)
        out_specs.append(pl.BlockSpec((None, 2, D_INNER, D_STATE), lambda b, c: (b, 0, 0, 0)))
        out_shape.append(jax.ShapeDtypeStruct((batch, 2, D_INNER, D_STATE), F32))
    elif emit_state:
        out_specs.append(pl.BlockSpec((None, D_INNER, D_STATE), lambda b, c: (b, 0, 0)))
        out_shape.append(jax.ShapeDtypeStruct((batch, D_INNER, D_STATE), F32))
    scratch = [pltpu.VMEM((SSM_GROUPS, D_STATE, GROUP_WIDTH), F32),
               pltpu.VMEM((2, 6, q, q), BF16)]
    if combine is not None:
        scratch.append(pltpu.VMEM((SSM_GROUPS, q, GROUP_WIDTH), F32))
    return pl.pallas_call(
        functools.partial(_ssd_kernel, direction=direction, has_h0=h0 is not None,
                          emit_state=emit_state, combine=combine is not None),
        grid=(batch, nc),
        in_specs=in_specs,
        out_specs=out_specs,
        out_shape=out_shape,
        scratch_shapes=scratch,
        compiler_params=_params(("parallel", "arbitrary")),
        name="ssd_fwd" if direction == 0 else "ssd_bwd",
    )(*args)


def _trunk(h, mods, w, *, row0, n_cond, batch, seq, cache_k=None, cache_v=None, state=None):
    context = cache_k is None
    outs = {}
    for l in range(DEPTH):
        cond = _Cond(l, row0, n_cond, seq)
        norm_g = w["norm_g"]
        h = _ffn(h, mods, norm_g, w["ffn_w_in"], w["ffn_w_out"], cond, half=0)
        if l % 2 == 0:
            e = l // 2
            pool_y, q, k, v = _even_in(h, mods, norm_g, w["mix_w_in"], w["pool_w"], w["pool_scale"],
                                       w["qk_norm_g"], cond, e=e, rope=not context)
            attn = _attention(q, k, v, cache_k, cache_v, e=e, batch=batch, seq=seq)
            h = _even_out(pool_y, attn, w["mix_w_out"], h, mods, norm_g, cond, e=e)
            outs["k"], outs["v"] = k, v
        else:
            o = l // 2
            z, xbc, dt = _odd_in(h, mods, norm_g, w["ssm_w_in"], w["ssm_conv_w"], w["ssm_conv_b"],
                                 w["ssm_dt_bias"], cond, o=o)
            scan = functools.partial(_ssd, xbc, dt, w["ssm_A_log"], state, o=o, batch=batch, seq=seq)
            fwd = scan(direction=0, emit_state=context)
            bwd = scan(direction=1, emit_state=context, other_state=fwd[1] if context else None,
                       combine=(fwd[0], z, w["ssm_D"], w["ssm_norm_g"]))
            if context:
                outs["ssm"] = bwd[1]
            h = _odd_out(bwd[0], w["ssm_w_out"], h, mods, norm_g, cond, o=o)
        h = _ffn(h, mods, norm_g, w["ffn_w_in"], w["ffn_w_out"], cond, half=1)
    return h, outs


def kernel(x_prompt, x_sample, cache_k, cache_v, state_ssm, c, c_ctx, ada_w, ada_b, norm_g, ffn_w_in,
           ffn_w_out, mix_w_in, pool_w, pool_scale, qk_norm_g, mix_w_out, ssm_w_in, ssm_conv_w,
           ssm_conv_b, ssm_dt_bias, ssm_A_log, ssm_D, ssm_norm_g, ssm_w_out):
    batch, seq, _ = x_prompt.shape
    dec_batch, dec_seq, _ = x_sample.shape
    n_even, n_odd = mix_w_in.shape[0], ssm_w_in.shape[0]
    past = cache_k.shape[2]

    cond = jnp.zeros((COND_ROWS, D_MODEL), F32).at[0].set(c_ctx).at[1:1 + dec_batch].set(c)
    mods = _ada_mods(cond, ada_w, ada_b).reshape(DEPTH, COND_ROWS, N_MOD, D_MODEL)

    w = dict(
        norm_g=norm_g,
        ffn_w_in=ffn_w_in.astype(BF16), ffn_w_out=ffn_w_out.astype(BF16),
        mix_w_in=mix_w_in.astype(BF16), pool_w=pool_w.astype(BF16),
        pool_scale=pool_scale.reshape(n_even, 1, POOL_WIDTH), qk_norm_g=qk_norm_g,
        mix_w_out=mix_w_out.astype(BF16),
        ssm_w_in=ssm_w_in.astype(BF16),
        ssm_conv_w=jnp.swapaxes(ssm_conv_w, 1, 2), ssm_conv_b=ssm_conv_b.reshape(n_odd, 1, CONV_DIM),
        ssm_dt_bias=ssm_dt_bias.reshape(n_odd, 1, 2 * SSM_HEADS),
        ssm_A_log=ssm_A_log.reshape(n_odd, 1, 2 * SSM_HEADS),
        ssm_D=jnp.repeat(ssm_D, SSM_HEADDIM, axis=-1).reshape(n_odd, 2, SSM_GROUPS, 1, GROUP_WIDTH),
        ssm_norm_g=ssm_norm_g.reshape(n_odd, SSM_GROUPS, 1, GROUP_WIDTH),
        ssm_w_out=ssm_w_out.astype(BF16),
    )

    y_prompt, ctx = _trunk(x_prompt.reshape(batch * seq, D_MODEL), mods, w, row0=0, n_cond=1,
                           batch=batch, seq=seq)
    y_sample, _ = _trunk(x_sample.reshape(dec_batch * dec_seq, D_MODEL), mods, w, row0=1, n_cond=dec_batch,
                         batch=dec_batch, seq=dec_seq,
                         cache_k=cache_k.reshape(dec_batch, n_even, past, KV_WIDTH),
                         cache_v=cache_v.reshape(dec_batch, n_even, past, KV_WIDTH),
                         state=state_ssm.reshape(dec_batch, n_odd, 2, D_INNER, D_STATE))

    new_k = ctx["k"].reshape(batch, n_even, seq, N_KV_HEADS, HEAD_DIM)
    new_v = ctx["v"].reshape(batch, n_even, seq, N_KV_HEADS, HEAD_DIM)
    new_ssm = ctx["ssm"].reshape(batch, n_odd, 2, SSM_HEADS, SSM_HEADDIM, D_STATE)
    return (y_prompt.reshape(batch, seq, D_MODEL), y_sample.reshape(dec_batch, dec_seq, D_MODEL),
            new_k, new_v, new_ssm)
```

```python
import functools

import jax
import jax.numpy as jnp
from jax import lax
from jax.experimental import pallas as pl
from jax.experimental.pallas import tpu as pltpu

F32 = jnp.float32
BF16 = jnp.bfloat16

D_MODEL = 2048
DEPTH = 2
GRID_W = 64
EPS = 1e-6
N_MOD = 9
N_NORM = 6
D_FF = 5632
POOL_WINDOWS = (2, 4, 8, 16)
POOL_WIDTH = 1024
POOL_GROUP_DIM = 256
HEAD_DIM = 128
N_KV_HEADS = 2
Q_PER_KV = 4
ATTN_WIDTH = 1024
KV_WIDTH = 256
MIX_IN = 2560
ROPE_THETA = 10000.0
D_INNER = 4096
SSM_HEADDIM = 64
SSM_HEADS = 64
SSM_GROUPS = 8
HEADS_PER_GROUP = 8
D_STATE = 128
SSM_CHUNK = 128
CONV_DIM = 6144
GROUP_WIDTH = HEADS_PER_GROUP * SSM_HEADDIM

LANES = 128
VMEM_LIMIT = 60 * 1024 * 1024
COND_ROWS = 16


def _params(sem):
    return pltpu.CompilerParams(dimension_semantics=sem, vmem_limit_bytes=VMEM_LIMIT)


def _rms(x, g):
    ms = jnp.mean(x * x, axis=-1, keepdims=True)
    return x * lax.rsqrt(ms + EPS) * g


def _silu(x):
    return x * jax.nn.sigmoid(x)


ROW_BLOCK = 16


def _modulate_rows(h_ref, g, shift, scale, u_ref):
    gain = g * (1 + scale)
    for r in range(0, h_ref.shape[0], ROW_BLOCK):
        x = h_ref[r:r + ROW_BLOCK]
        ms = jnp.mean(x * x, axis=-1, keepdims=True)
        u_ref[r:r + ROW_BLOCK] = (x * lax.rsqrt(ms + EPS) * gain + shift).astype(BF16)


def _residual_rows(h_ref, y_ref, g, gate, out_ref):
    gain = g * gate
    for r in range(0, h_ref.shape[0], ROW_BLOCK):
        y = y_ref[r:r + ROW_BLOCK]
        ms = jnp.mean(y * y, axis=-1, keepdims=True)
        out_ref[r:r + ROW_BLOCK] = h_ref[r:r + ROW_BLOCK] + y * lax.rsqrt(ms + EPS) * gain


class _Cond:
    def __init__(self, layer, row0, n_cond, seq):
        self.layer, self.row0, self.n_cond, self.seq = layer, row0, n_cond, seq

    def specs(self, tm):
        layer, row0 = self.layer, self.row0
        if self.n_cond == 1:
            row = lambda i: row0
        else:
            assert self.seq % tm == 0
            per_seq = self.seq // tm
            row = lambda i: row0 + i // per_seq
        return [pl.BlockSpec((None, None, N_MOD, D_MODEL), lambda i, *_: (layer, row(i), 0, 0)),
                pl.BlockSpec((None, N_NORM, D_MODEL), lambda *_: (layer, 0, 0))]


def _mods_kernel(cond_ref, w_ref, b_ref, out_ref):
    sc = _silu(cond_ref[...]).astype(BF16)
    w = w_ref[...].astype(BF16)
    out_ref[...] = jnp.dot(sc, w, preferred_element_type=F32) + b_ref[...]


def _ada_mods(cond, ada_w, ada_b):
    tn = 1024
    n_out = N_MOD * D_MODEL
    return pl.pallas_call(
        _mods_kernel,
        grid=(DEPTH, n_out // tn),
        in_specs=[
            pl.BlockSpec((COND_ROWS, D_MODEL), lambda l, n: (0, 0)),
            pl.BlockSpec((None, D_MODEL, tn), lambda l, n: (l, 0, n)),
            pl.BlockSpec((None, 1, tn), lambda l, n: (l, 0, n)),
        ],
        out_specs=pl.BlockSpec((None, COND_ROWS, tn), lambda l, n: (l, 0, n)),
        out_shape=jax.ShapeDtypeStruct((DEPTH, COND_ROWS, n_out), F32),
        compiler_params=_params(("parallel", "parallel")),
        name="ada_mods",
    )(cond, ada_w, ada_b.reshape(DEPTH, 1, n_out))


FFN_SUB = 256


def _ffn_kernel(h_ref, mods_ref, g_ref, wa_ref, wb_ref, wo_ref, out_ref, u_ref, ab_ref, hid_ref,
                *, mod0, g0):
    j = pl.program_id(1)
    n_sub = wo_ref.shape[0] // FFN_SUB

    def gate_rows(s):
        for r in range(0, ab_ref.shape[1], ROW_BLOCK):
            ab = ab_ref[s, r:r + ROW_BLOCK]
            hid_ref[s, r:r + ROW_BLOCK] = (_silu(ab[:, :FFN_SUB]) * ab[:, FFN_SUB:]).astype(BF16)

    def partial_out():
        u = u_ref[...]
        for s in range(n_sub):
            cols = slice(s * FFN_SUB, (s + 1) * FFN_SUB)
            w_ab = jnp.concatenate([wa_ref[:, cols], wb_ref[:, cols]], axis=1)
            ab_ref[s] = jnp.dot(u, w_ab, preferred_element_type=F32)
            gate_rows(s)
        y = jnp.dot(hid_ref[0], wo_ref[:FFN_SUB], preferred_element_type=F32)
        for s in range(1, n_sub):
            y = y + jnp.dot(hid_ref[s], wo_ref[s * FFN_SUB:(s + 1) * FFN_SUB], preferred_element_type=F32)
        return y

    @pl.when(j == 0)
    def _():
        _modulate_rows(h_ref, g_ref[g0:g0 + 1], mods_ref[mod0:mod0 + 1], mods_ref[mod0 + 1:mod0 + 2], u_ref)
        out_ref[...] = partial_out()

    @pl.when(j > 0)
    def _():
        out_ref[...] += partial_out()

    @pl.when(j == pl.num_programs(1) - 1)
    def _():
        _residual_rows(h_ref, out_ref, g_ref[g0 + 1:g0 + 2], 0.5 * mods_ref[mod0 + 2:mod0 + 3], out_ref)


def _ffn(h, mods, norm_g, w_in, w_out, cond, *, half):
    tokens = h.shape[0]
    tm, tf = 1024, 512
    nf = D_FF // tf
    layer = cond.layer
    return pl.pallas_call(
        functools.partial(_ffn_kernel, mod0=6 * half, g0=4 * half),
        grid=(tokens // tm, nf),
        in_specs=[
            pl.BlockSpec((tm, D_MODEL), lambda i, j: (i, 0)),
            *cond.specs(tm),
            pl.BlockSpec((None, None, D_MODEL, tf), lambda i, j: (layer, half, 0, j)),
            pl.BlockSpec((None, None, D_MODEL, tf), lambda i, j: (layer, half, 0, nf + j)),
            pl.BlockSpec((None, None, tf, D_MODEL), lambda i, j: (layer, half, j, 0)),
        ],
        out_specs=pl.BlockSpec((tm, D_MODEL), lambda i, j: (i, 0)),
        out_shape=jax.ShapeDtypeStruct((tokens, D_MODEL), F32),
        scratch_shapes=[pltpu.VMEM((tm, D_MODEL), BF16),
                        pltpu.VMEM((tf // FFN_SUB, tm, 2 * FFN_SUB), F32),
                        pltpu.VMEM((tf // FFN_SUB, tm, FFN_SUB), BF16)],
        compiler_params=_params(("parallel", "arbitrary")),
        name="ffn",
    )(h, mods, norm_g, w_in, w_in, w_out)


SUBLANES = 8


POOL_ROWS = 64


def _halo_block(p_ref, r, n_rows, cols, seq):
    zeros = jnp.zeros((SUBLANES, cols.stop - cols.start), F32)
    before = zeros if r % seq == 0 else p_ref[r - SUBLANES:r, cols]
    after = zeros if (r + n_rows) % seq == 0 else p_ref[r + n_rows:r + n_rows + SUBLANES, cols]
    return jnp.concatenate([before, p_ref[r:r + n_rows, cols], after], axis=0)


def _pool_delta_rows(p_ref, cols, window, seq, d_ref):
    half = window // 2
    assert half <= SUBLANES and seq % POOL_ROWS == 0
    n = p_ref.shape[0]
    rows = POOL_ROWS + 2 * SUBLANES
    row = lax.broadcasted_iota(jnp.int32, (POOL_ROWS, 1), 0)
    for r in range(0, n, POOL_ROWS):
        blk = _halo_block(p_ref, r, POOL_ROWS, cols, seq)
        fwd = blk
        bwd = blk
        m = 1
        while m < half:
            fwd = fwd + pltpu.roll(fwd, rows - m, axis=0)
            bwd = bwd + pltpu.roll(bwd, m, axis=0)
            m *= 2
        total = (fwd + pltpu.roll(bwd, 1, axis=0))[SUBLANES:SUBLANES + POOL_ROWS]
        pos = row + r % seq
        count = jnp.minimum(pos + half, seq) - jnp.maximum(pos - half, 0)
        d = total / count.astype(F32) - blk[SUBLANES:SUBLANES + POOL_ROWS]
        d_ref[r:r + POOL_ROWS, cols] = d.astype(BF16)


def _rope(x, cos, sin_lo, sin_hi):
    quarter = HEAD_DIM // 4
    return (x * cos + pltpu.roll(x, HEAD_DIM - quarter, axis=1) * sin_lo
            + pltpu.roll(x, quarter, axis=1) * sin_hi)


HEAD_ROWS_ROPE = 256


def _even_in_kernel(h_ref, mods_ref, g_ref, w_ref, pw_ref, ps_ref, qkg_ref, *rest, seq, rope):
    if rope:
        cos_ref, slo_ref, shi_ref, pool_ref, q_ref, k_ref, v_ref, u_ref, p_ref, d_ref = rest
    else:
        pool_ref, q_ref, k_ref, v_ref, u_ref, p_ref, d_ref = rest
    j = pl.program_id(1)
    tm = h_ref.shape[0]
    n_pool = POOL_WIDTH // p_ref.shape[2]
    n_q = ATTN_WIDTH // p_ref.shape[2]

    def project(chunk):
        p_ref[chunk % 2] = jnp.dot(u_ref[...], w_ref[...], preferred_element_type=F32)

    groups = p_ref.shape[2] // POOL_GROUP_DIM

    def pool_deltas(chunk):
        for gi in range(groups):
            cols = slice(gi * POOL_GROUP_DIM, (gi + 1) * POOL_GROUP_DIM)
            _pool_delta_rows(p_ref.at[chunk % 2], cols, POOL_WINDOWS[groups * chunk + gi], seq, d_ref)

    def pool_mix():
        for gi in range(groups):
            cols = slice(gi * POOL_GROUP_DIM, (gi + 1) * POOL_GROUP_DIM)
            y = jnp.dot(d_ref[:, cols], pw_ref[gi], preferred_element_type=F32)
            pool_ref[:, cols] = (y * ps_ref[:, cols]).astype(BF16)

    def heads_epilogue(chunk, n_heads, gain, dst_ref):
        src = p_ref.at[chunk % 2]
        step = HEAD_ROWS_ROPE if rope else ROW_BLOCK
        for r in range(0, tm, step):
            rows = slice(r, r + step)
            for hd in range(n_heads):
                cols = slice(hd * HEAD_DIM, (hd + 1) * HEAD_DIM)
                x = _rms(src[rows, cols], gain)
                if rope:
                    x = _rope(x, cos_ref[rows], slo_ref[rows], shi_ref[rows])
                dst_ref[rows, cols] = x.astype(dst_ref.dtype)

    def kv_epilogue(chunk):
        heads_epilogue(chunk, N_KV_HEADS, qkg_ref[1:2], k_ref)
        src = p_ref.at[chunk % 2]
        for r in range(0, tm, ROW_BLOCK):
            v_ref[r:r + ROW_BLOCK] = src[r:r + ROW_BLOCK, KV_WIDTH:]

    @pl.when(j == 0)
    def _():
        _modulate_rows(h_ref, g_ref[2:3], mods_ref[3:4], mods_ref[4:5], u_ref)
        project(0)

    for chunk in range(1, n_pool + n_q + 2):
        @pl.when(j == chunk)
        def _(chunk=chunk):
            done = chunk - 1
            if done < n_pool:
                pool_deltas(done)
            elif done < n_pool + n_q:
                heads_epilogue(done, Q_PER_KV, qkg_ref[0:1], q_ref)
            else:
                kv_epilogue(done)
            if chunk <= n_pool + n_q:
                project(chunk)
            if done < n_pool:
                pool_mix()


def _rope_tables(seq):
    pos = jnp.arange(seq)
    row = (pos // GRID_W).astype(F32)
    col = (pos % GRID_W).astype(F32)
    quarter = HEAD_DIM // 4
    inv_freq = ROPE_THETA ** (-jnp.arange(quarter, dtype=F32) / quarter)
    ang_r = row[:, None] * inv_freq[None]
    ang_c = col[:, None] * inv_freq[None]
    zero = jnp.zeros_like(ang_r)
    cos = jnp.concatenate([jnp.cos(ang_r), jnp.cos(ang_r), jnp.cos(ang_c), jnp.cos(ang_c)], axis=-1)
    sin_lo = jnp.concatenate([-jnp.sin(ang_r), zero, -jnp.sin(ang_c), zero], axis=-1)
    sin_hi = jnp.concatenate([zero, jnp.sin(ang_r), zero, jnp.sin(ang_c)], axis=-1)
    return cos, sin_lo, sin_hi


def _even_in(h, mods, norm_g, w_in, pool_w, pool_scale, qk_g, cond, *, e, rope):
    tokens = h.shape[0]
    seq = cond.seq
    tm, tn = 1024, 512
    assert tm % seq == 0 and tokens % tm == 0
    n_pool = POOL_WIDTH // tn
    n_q = ATTN_WIDTH // tn
    n_chunks = MIX_IN // tn
    pool_idx = lambda j: jnp.clip(j - 1, 0, n_pool - 1)
    q_idx = lambda j: jnp.clip(j - 1 - n_pool, 0, n_q - 1)
    in_specs = [
        pl.BlockSpec((tm, D_MODEL), lambda i, j: (i, 0)),
        *cond.specs(tm),
        pl.BlockSpec((None, D_MODEL, tn), lambda i, j: (e, 0, jnp.minimum(j, n_chunks - 1))),
        pl.BlockSpec((None, 2, POOL_GROUP_DIM, POOL_GROUP_DIM), lambda i, j: (e, pool_idx(j), 0, 0)),
        pl.BlockSpec((None, 1, tn), lambda i, j: (e, 0, pool_idx(j))),
        pl.BlockSpec((None, 2, HEAD_DIM), lambda i, j: (e, 0, 0)),
    ]
    args = [h, mods, norm_g, w_in, pool_w, pool_scale, qk_g]
    if rope:
        assert tm == seq
        in_specs += [pl.BlockSpec((tm, HEAD_DIM), lambda i, j: (0, 0))] * 3
        args += list(_rope_tables(seq))
    return pl.pallas_call(
        functools.partial(_even_in_kernel, seq=seq, rope=rope),
        grid=(tokens // tm, n_chunks + 1),
        in_specs=in_specs,
        out_specs=[
            pl.BlockSpec((tm, tn), lambda i, j: (i, pool_idx(j))),
            pl.BlockSpec((tm, tn), lambda i, j: (i, q_idx(j))),
            pl.BlockSpec((tm, KV_WIDTH), lambda i, j: (i, 0)),
            pl.BlockSpec((tm, KV_WIDTH), lambda i, j: (i, 0)),
        ],
        out_shape=[
            jax.ShapeDtypeStruct((tokens, POOL_WIDTH), BF16),
            jax.ShapeDtypeStruct((tokens, ATTN_WIDTH), BF16),
            jax.ShapeDtypeStruct((tokens, KV_WIDTH), F32),
            jax.ShapeDtypeStruct((tokens, KV_WIDTH), F32),
        ],
        scratch_shapes=[pltpu.VMEM((tm, D_MODEL), BF16), pltpu.VMEM((2, tm, tn), F32),
                        pltpu.VMEM((tm, tn), BF16)],
        compiler_params=_params(("parallel", "arbitrary")),
        name="even_in",
    )(*args)


def _attn_kernel(q_ref, k_ref, v_ref, *rest, cached):
    if cached:
        ck_ref, cv_ref, o_ref = rest
    else:
        (o_ref,) = rest
    scale = HEAD_DIM ** -0.5
    nt = (((1,), (1,)), ((), ()))
    kb = k_ref[...].astype(BF16)
    vb = v_ref[...].astype(BF16)
    if cached:
        ckb = ck_ref[...].astype(BF16)
        cvb = cv_ref[...].astype(BF16)
    for hd in range(Q_PER_KV):
        sl = slice(hd * HEAD_DIM, (hd + 1) * HEAD_DIM)
        q = q_ref[:, sl]
        s = lax.dot_general(q, kb, nt, preferred_element_type=F32) * scale
        m = jnp.max(s, axis=-1, keepdims=True)
        if cached:
            sc = lax.dot_general(q, ckb, nt, preferred_element_type=F32) * scale
            m = jnp.maximum(m, jnp.max(sc, axis=-1, keepdims=True))
        e = jnp.exp(s - m)
        denom = jnp.sum(e, axis=-1, keepdims=True)
        o = jnp.dot(e.astype(BF16), vb, preferred_element_type=F32)
        if cached:
            ec = jnp.exp(sc - m)
            denom = denom + jnp.sum(ec, axis=-1, keepdims=True)
            o = o + jnp.dot(ec.astype(BF16), cvb, preferred_element_type=F32)
        o_ref[:, sl] = (o / denom).astype(BF16)


def _attention(q, k, v, cache_k, cache_v, *, e, batch, seq):
    tokens = q.shape[0]
    tq = 256
    nq = seq // tq
    gw = Q_PER_KV * HEAD_DIM
    cached = cache_k is not None
    in_specs = [
        pl.BlockSpec((tq, gw), lambda b, g, i: (b * nq + i, g)),
        pl.BlockSpec((seq, HEAD_DIM), lambda b, g, i: (b, g)),
        pl.BlockSpec((seq, HEAD_DIM), lambda b, g, i: (b, g)),
    ]
    args = [q, k, v]
    if cached:
        past = cache_k.shape[2]
        in_specs += [pl.BlockSpec((None, None, past, HEAD_DIM), lambda b, g, i: (b, e, 0, g))] * 2
        args += [cache_k, cache_v]
    return pl.pallas_call(
        functools.partial(_attn_kernel, cached=cached),
        grid=(batch, N_KV_HEADS, nq),
        in_specs=in_specs,
        out_specs=pl.BlockSpec((tq, gw), lambda b, g, i: (b * nq + i, g)),
        out_shape=jax.ShapeDtypeStruct((tokens, ATTN_WIDTH), BF16),
        compiler_params=_params(("parallel", "parallel", "parallel")),
        name="attention",
    )(*args)


def _even_out_kernel(pool_ref, attn_ref, w_ref, h_ref, mods_ref, g_ref, out_ref):
    y = jnp.dot(pool_ref[...], w_ref[:POOL_WIDTH], preferred_element_type=F32)
    y = y + jnp.dot(attn_ref[...], w_ref[POOL_WIDTH:], preferred_element_type=F32)
    out_ref[...] = y
    _residual_rows(h_ref, out_ref, g_ref[3:4], mods_ref[5:6], out_ref)


def _even_out(pool_y, attn, w_out, h, mods, norm_g, cond, *, e):
    tokens = h.shape[0]
    tm = 512
    return pl.pallas_call(
        _even_out_kernel,
        grid=(tokens // tm,),
        in_specs=[
            pl.BlockSpec((tm, POOL_WIDTH), lambda i: (i, 0)),
            pl.BlockSpec((tm, ATTN_WIDTH), lambda i: (i, 0)),
            pl.BlockSpec((None, POOL_WIDTH + ATTN_WIDTH, D_MODEL), lambda i: (e, 0, 0)),
            pl.BlockSpec((tm, D_MODEL), lambda i: (i, 0)),
            *cond.specs(tm),
        ],
        out_specs=pl.BlockSpec((tm, D_MODEL), lambda i: (i, 0)),
        out_shape=jax.ShapeDtypeStruct((tokens, D_MODEL), F32),
        compiler_params=_params(("parallel",)),
        name="even_out",
    )(pool_y, attn, w_out, h, mods, norm_g)


def _odd_out_kernel(x_ref, w_ref, h_ref, mods_ref, g_ref, out_ref):
    k = pl.program_id(1)
    y = jnp.dot(x_ref[0], w_ref[:GROUP_WIDTH], preferred_element_type=F32)
    for g in range(1, x_ref.shape[0]):
        y = y + jnp.dot(x_ref[g], w_ref[g * GROUP_WIDTH:(g + 1) * GROUP_WIDTH], preferred_element_type=F32)

    @pl.when(k == 0)
    def _():
        out_ref[...] = y

    @pl.when(k == pl.num_programs(1) - 1)
    def _():
        out_ref[...] += y
        _residual_rows(h_ref, out_ref, g_ref[3:4], mods_ref[5:6], out_ref)


def _odd_out(yn, w_out, h, mods, norm_g, cond, *, o):
    tokens = h.shape[0]
    tm, nk = 512, 2
    gk = SSM_GROUPS // nk
    return pl.pallas_call(
        _odd_out_kernel,
        grid=(tokens // tm, nk),
        in_specs=[
            pl.BlockSpec((gk, tm, GROUP_WIDTH), lambda i, k: (k, i, 0)),
            pl.BlockSpec((None, gk * GROUP_WIDTH, D_MODEL), lambda i, k: (o, k, 0)),
            pl.BlockSpec((tm, D_MODEL), lambda i, k: (i, 0)),
            *cond.specs(tm),
        ],
        out_specs=pl.BlockSpec((tm, D_MODEL), lambda i, k: (i, 0)),
        out_shape=jax.ShapeDtypeStruct((tokens, D_MODEL), F32),
        compiler_params=_params(("parallel", "arbitrary")),
        name="odd_out",
    )(yn, w_out, h, mods, norm_g)


N_Z = D_INNER // GROUP_WIDTH
N_XBC = CONV_DIM // GROUP_WIDTH
X_CHUNKS = D_INNER // GROUP_WIDTH
BC_CHUNKS = SSM_GROUPS * D_STATE // GROUP_WIDTH
GROUPS_PER_CHUNK = GROUP_WIDTH // D_STATE
ODD_IN_CHUNKS = 2


def _conv_silu_rows(p_ref, cols, cw_ref, cb_ref, out_ref, seq):
    n = p_ref.shape[0]
    row = lax.broadcasted_iota(jnp.int32, (ROW_BLOCK, 1), 0)
    w_prev, w_cur, w_next, bias = cw_ref[0:1, cols], cw_ref[1:2, cols], cw_ref[2:3, cols], cb_ref[:, cols]
    for r in range(0, n, ROW_BLOCK):
        lo, hi = max(r - SUBLANES, 0), min(r + ROW_BLOCK + SUBLANES, n)
        blk = p_ref[lo:hi, cols]
        cur = blk[r - lo:r - lo + ROW_BLOCK]
        prev = pltpu.roll(blk, 1, axis=0)[r - lo:r - lo + ROW_BLOCK]
        nxt = pltpu.roll(blk, hi - lo - 1, axis=0)[r - lo:r - lo + ROW_BLOCK]
        if r % seq == 0:
            prev = jnp.where(row == 0, 0.0, prev)
        if (r + ROW_BLOCK) % seq == 0:
            nxt = jnp.where(row == ROW_BLOCK - 1, 0.0, nxt)
        out_ref[r:r + ROW_BLOCK] = _silu(prev * w_prev + cur * w_cur + nxt * w_next + bias)


def _odd_in_kernel(h_ref, mods_ref, g_ref, w_ref, wdt_ref, cw_ref, cb_ref, dtb_ref,
                   z_ref, xbc_ref, dt_ref, u_ref, p_ref, *, seq):
    j = pl.program_id(1)

    def project():
        return jnp.dot(u_ref[...], w_ref[...], preferred_element_type=F32)

    def chunk_cols(c):
        return slice(c * GROUP_WIDTH, (c + 1) * GROUP_WIDTH)

    @pl.when(j == 0)
    def _():
        _modulate_rows(h_ref, g_ref[2:3], mods_ref[3:4], mods_ref[4:5], u_ref)
        raw = jnp.dot(u_ref[...], wdt_ref[...], preferred_element_type=F32) + dtb_ref[...]
        dt_ref[...] = jnp.maximum(raw, 0.0) + jnp.log1p(jnp.exp(-jnp.abs(raw)))

    @pl.when(j < N_Z // ODD_IN_CHUNKS)
    def _():
        p = project()
        for c in range(ODD_IN_CHUNKS):
            z_ref[c] = p[:, chunk_cols(c)]

    @pl.when(j >= N_Z // ODD_IN_CHUNKS)
    def _():
        p_ref[...] = project()
        for c in range(ODD_IN_CHUNKS):
            _conv_silu_rows(p_ref, chunk_cols(c), cw_ref, cb_ref, xbc_ref.at[c], seq)


def _odd_in(h, mods, norm_g, w_in, conv_w, conv_b, dt_bias, cond, *, o):
    tokens = h.shape[0]
    seq = cond.seq
    tm, tn = 1024, ODD_IN_CHUNKS * GROUP_WIDTH
    assert tm % seq == 0 and tokens % tm == 0
    assert N_Z % ODD_IN_CHUNKS == 0 and N_XBC % ODD_IN_CHUNKS == 0
    nz = N_Z // ODD_IN_CHUNKS
    dt_block = (D_INNER + CONV_DIM) // LANES
    conv_idx = lambda j: jnp.maximum(j - nz, 0)
    return pl.pallas_call(
        functools.partial(_odd_in_kernel, seq=seq),
        grid=(tokens // tm, (N_Z + N_XBC) // ODD_IN_CHUNKS),
        in_specs=[
            pl.BlockSpec((tm, D_MODEL), lambda i, j: (i, 0)),
            *cond.specs(tm),
            pl.BlockSpec((None, D_MODEL, tn), lambda i, j: (o, 0, j)),
            pl.BlockSpec((None, D_MODEL, 2 * SSM_HEADS), lambda i, j: (o, 0, dt_block)),
            pl.BlockSpec((None, 3, tn), lambda i, j: (o, 0, conv_idx(j))),
            pl.BlockSpec((None, 1, tn), lambda i, j: (o, 0, conv_idx(j))),
            pl.BlockSpec((None, 1, 2 * SSM_HEADS), lambda i, j: (o, 0, 0)),
        ],
        out_specs=[
            pl.BlockSpec((ODD_IN_CHUNKS, tm, GROUP_WIDTH), lambda i, j: (jnp.minimum(j, nz - 1), i, 0)),
            pl.BlockSpec((ODD_IN_CHUNKS, tm, GROUP_WIDTH), lambda i, j: (conv_idx(j), i, 0)),
            pl.BlockSpec((tm, 2 * SSM_HEADS), lambda i, j: (i, 0)),
        ],
        out_shape=[
            jax.ShapeDtypeStruct((N_Z, tokens, GROUP_WIDTH), F32),
            jax.ShapeDtypeStruct((N_XBC, tokens, GROUP_WIDTH), F32),
            jax.ShapeDtypeStruct((tokens, 2 * SSM_HEADS), F32),
        ],
        scratch_shapes=[pltpu.VMEM((tm, D_MODEL), BF16), pltpu.VMEM((tm, tn), F32)],
        compiler_params=_params(("parallel", "arbitrary")),
        name="odd_in",
    )(h, mods, norm_g, w_in, w_in, conv_w, conv_b, dt_bias)


def _split3(x):
    x1 = x.astype(BF16)
    r1 = x - x1.astype(F32)
    x2 = r1.astype(BF16)
    x3 = (r1 - x2.astype(F32)).astype(BF16)
    return x1, x2, x3


def _ssd_kernel(*refs, direction, has_h0, emit_state, combine):
    refs = list(refs)
    x_ref, b_ref, c_ref, dt_ref, alog_ref = refs[:5]
    del refs[:5]
    h0_ref = refs.pop(0) if has_h0 else None
    if combine:
        yf_ref, z_ref, dskip_ref, ng_ref = refs[:4]
        del refs[:4]
    other_state_ref = refs.pop(0) if (emit_state and combine) else None
    y_ref = refs.pop(0)
    state_out_ref = refs.pop(0) if emit_state else None
    ht_ref = refs.pop(0)
    yb_ref = refs.pop(0) if combine else None

    c = pl.program_id(1)
    q = SSM_CHUNK
    pairs = GROUP_WIDTH // LANES

    @pl.when(c == 0)
    def _():
        if has_h0:
            for g in range(SSM_GROUPS):
                for pr in range(pairs):
                    r0 = g * GROUP_WIDTH + pr * LANES
                    ht_ref[g, :, pr * LANES:(pr + 1) * LANES] = h0_ref[r0:r0 + LANES, :].T
        else:
            ht_ref[...] = jnp.zeros_like(ht_ref)

    ii = lax.broadcasted_iota(jnp.int32, (q, q), 0)
    jj = lax.broadcasted_iota(jnp.int32, (q, q), 1)
    causal = (jj <= ii) if direction == 0 else (jj >= ii)
    tri = jnp.where(causal, 1.0, 0.0).astype(BF16)
    low_lane = lax.broadcasted_iota(jnp.int32, (1, LANES), 1) < SSM_HEADDIM

    dt = dt_ref[...]
    dta = dt * (-jnp.exp(alog_ref[...]))
    a = sum(jnp.dot(tri, part, preferred_element_type=F32) for part in _split3(dta))
    a_end = a[q - 1:q] if direction == 0 else a[0:1]
    a_t = a.T
    dt_t = dt.T
    s_t = (dt * jnp.exp(a_end - a)).T
    end_scale = jnp.exp(a_end)

    for g in range(SSM_GROUPS):
        bc_lanes = slice((g % GROUPS_PER_CHUNK) * D_STATE, (g % GROUPS_PER_CHUNK + 1) * D_STATE)
        bg_t = b_ref[g // GROUPS_PER_CHUNK, :, bc_lanes].T
        cg32 = c_ref[g // GROUPS_PER_CHUNK, :, bc_lanes]
        cb = jnp.dot(cg32.astype(BF16), bg_t.astype(BF16), preferred_element_type=F32)
        for pr in range(pairs):
            lanes = slice(pr * LANES, (pr + 1) * LANES)
            xp = x_ref[g, :, lanes].astype(BF16)
            hp = ht_ref[g, :, lanes]
            hpb = hp.astype(BF16)
            ys, ds, scales = [], [], []
            for e in range(2):
                hl = direction * SSM_HEADS + g * HEADS_PER_GROUP + 2 * pr + e
                col = jnp.broadcast_to(a[:, hl:hl + 1], (q, q))
                decay = jnp.exp(jnp.where(causal, col - a_t[hl:hl + 1, :], -jnp.inf))
                w = (cb * decay * dt_t[hl:hl + 1, :]).astype(BF16)
                cdec = (cg32 * jnp.exp(col)).astype(BF16)
                ys.append(jnp.dot(w, xp, preferred_element_type=F32)
                          + jnp.dot(cdec, hpb, preferred_element_type=F32))
                bs = (bg_t * s_t[hl:hl + 1, :]).astype(BF16)
                ds.append(jnp.dot(bs, xp, preferred_element_type=F32))
                scales.append(jnp.broadcast_to(end_scale[:, hl:hl + 1], (1, LANES)))
            y_pair = jnp.where(low_lane, ys[0], ys[1])
            if combine:
                yb_ref[g, :, lanes] = y_pair
            else:
                y_ref[g, :, lanes] = y_pair
            ht_ref[g, :, lanes] = (hp * jnp.where(low_lane, scales[0], scales[1])
                                   + jnp.where(low_lane, ds[0], ds[1]))

    if combine:
        ssq = jnp.zeros((q, 1), F32)
        for g in range(SSM_GROUPS):
            xg = x_ref[g]
            y = (yf_ref[g] + dskip_ref[0, g] * xg) + (yb_ref[g] + dskip_ref[1, g] * xg)
            y = y * _silu(z_ref[g])
            yb_ref[g] = y
            ssq = ssq + jnp.sum(y * y, axis=-1, keepdims=True)
        inv = lax.rsqrt(ssq / D_INNER + EPS)
        for g in range(SSM_GROUPS):
            y_ref[g] = (yb_ref[g] * inv * ng_ref[g]).astype(BF16)

    if emit_state:
        @pl.when(c == pl.num_programs(1) - 1)
        def _():
            own = state_out_ref.at[direction] if combine else state_out_ref
            for g in range(SSM_GROUPS):
                for pr in range(pairs):
                    r0 = g * GROUP_WIDTH + pr * LANES
                    own[r0:r0 + LANES, :] = ht_ref[g, :, pr * LANES:(pr + 1) * LANES].T
            if combine:
                state_out_ref[1 - direction] = other_state_ref[...]


def _ssd(xbc, dt, a_log, h0, *, o, direction, batch, seq, emit_state, combine=None, other_state=None):
    tokens = xbc.shape[1]
    q = SSM_CHUNK
    nc = seq // q
    if direction == 0:
        chunk = lambda b, c: b * nc + c
    else:
        chunk = lambda b, c: b * nc + (nc - 1 - c)
    rows = lambda n, first: pl.BlockSpec((n, q, GROUP_WIDTH), lambda b, c: (first // n, chunk(b, c), 0))
    in_specs = [rows(X_CHUNKS, 0), rows(BC_CHUNKS, X_CHUNKS), rows(BC_CHUNKS, X_CHUNKS + BC_CHUNKS),
                pl.BlockSpec((q, 2 * SSM_HEADS), lambda b, c: (chunk(b, c), 0)),
                pl.BlockSpec((None, 1, 2 * SSM_HEADS), lambda b, c: (o, 0, 0))]
    args = [xbc, xbc, xbc, dt, a_log]
    if h0 is not None:
        in_specs.append(pl.BlockSpec((None, None, None, D_INNER, D_STATE), lambda b, c: (b, o, direction, 0, 0)))
        args.append(h0)
    if combine is not None:
        y_f, z, d_skip, norm_g = combine
        in_specs += [rows(X_CHUNKS, 0), rows(N_Z, 0),
                     pl.BlockSpec((None, 2, SSM_GROUPS, 1, GROUP_WIDTH), lambda b, c: (o, 0, 0, 0, 0)),
                     pl.BlockSpec((None, SSM_GROUPS, 1, GROUP_WIDTH), lambda b, c: (o, 0, 0, 0))]
        args += [y_f, z, d_skip, norm_g]
    out_specs = [rows(X_CHUNKS, 0)]
    out_shape = [jax.ShapeDtypeStruct((X_CHUNKS, tokens, GROUP_WIDTH), BF16 if combine is not None else F32)]
    if emit_state and combine is not None:
        in_specs.append(pl.BlockSpec((None, D_INNER, D_STATE), lambda b, c: (b, 0, 0)))
        args.append(other_state)
        out_specs.append(pl.BlockSpec((None, 2, D_INNER, D_STATE), lambda b, c: (b, 0, 0, 0)))
        out_shape.append(jax.ShapeDtypeStruct((batch, 2, D_INNER, D_STATE), F32))
    elif emit_state:
        out_specs.append(pl.BlockSpec((None, D_INNER, D_STATE), lambda b, c: (b, 0, 0)))
        out_shape.append(jax.ShapeDtypeStruct((batch, D_INNER, D_STATE), F32))
    scratch = [pltpu.VMEM((SSM_GROUPS, D_STATE, GROUP_WIDTH), F32)]
    if combine is not None:
        scratch.append(pltpu.VMEM((SSM_GROUPS, q, GROUP_WIDTH), F32))
    return pl.pallas_call(
        functools.partial(_ssd_kernel, direction=direction, has_h0=h0 is not None,
                          emit_state=emit_state, combine=combine is not None),
        grid=(batch, nc),
        in_specs=in_specs,
        out_specs=out_specs,
        out_shape=out_shape,
        scratch_shapes=scratch,
        compiler_params=_params(("parallel", "arbitrary")),
        name="ssd_fwd" if direction == 0 else "ssd_bwd",
    )(*args)


def _trunk(h, mods, w, *, row0, n_cond, batch, seq, cache_k=None, cache_v=None, state=None):
    context = cache_k is None
    outs = {}
    for l in range(DEPTH):
        cond = _Cond(l, row0, n_cond, seq)
        norm_g = w["norm_g"]
        h = _ffn(h, mods, norm_g, w["ffn_w_in"], w["ffn_w_out"], cond, half=0)
        if l % 2 == 0:
            e = l // 2
            pool_y, q, k, v = _even_in(h, mods, norm_g, w["mix_w_in"], w["pool_w"], w["pool_scale"],
                                       w["qk_norm_g"], cond, e=e, rope=not context)
            attn = _attention(q, k, v, cache_k, cache_v, e=e, batch=batch, seq=seq)
            h = _even_out(pool_y, attn, w["mix_w_out"], h, mods, norm_g, cond, e=e)
            outs["k"], outs["v"] = k, v
        else:
            o = l // 2
            z, xbc, dt = _odd_in(h, mods, norm_g, w["ssm_w_in"], w["ssm_conv_w"], w["ssm_conv_b"],
                                 w["ssm_dt_bias"], cond, o=o)
            scan = functools.partial(_ssd, xbc, dt, w["ssm_A_log"], state, o=o, batch=batch, seq=seq)
            fwd = scan(direction=0, emit_state=context)
            bwd = scan(direction=1, emit_state=context, other_state=fwd[1] if context else None,
                       combine=(fwd[0], z, w["ssm_D"], w["ssm_norm_g"]))
            if context:
                outs["ssm"] = bwd[1]
            h = _odd_out(bwd[0], w["ssm_w_out"], h, mods, norm_g, cond, o=o)
        h = _ffn(h, mods, norm_g, w["ffn_w_in"], w["ffn_w_out"], cond, half=1)
    return h, outs


def kernel(x_prompt, x_sample, cache_k, cache_v, state_ssm, c, c_ctx, ada_w, ada_b, norm_g, ffn_w_in,
           ffn_w_out, mix_w_in, pool_w, pool_scale, qk_norm_g, mix_w_out, ssm_w_in, ssm_conv_w,
           ssm_conv_b, ssm_dt_bias, ssm_A_log, ssm_D, ssm_norm_g, ssm_w_out):
    batch, seq, _ = x_prompt.shape
    dec_batch, dec_seq, _ = x_sample.shape
    n_even, n_odd = mix_w_in.shape[0], ssm_w_in.shape[0]
    past = cache_k.shape[2]

    cond = jnp.zeros((COND_ROWS, D_MODEL), F32).at[0].set(c_ctx).at[1:1 + dec_batch].set(c)
    mods = _ada_mods(cond, ada_w, ada_b).reshape(DEPTH, COND_ROWS, N_MOD, D_MODEL)

    w = dict(
        norm_g=norm_g,
        ffn_w_in=ffn_w_in.astype(BF16), ffn_w_out=ffn_w_out.astype(BF16),
        mix_w_in=mix_w_in.astype(BF16), pool_w=pool_w.astype(BF16),
        pool_scale=pool_scale.reshape(n_even, 1, POOL_WIDTH), qk_norm_g=qk_norm_g,
        mix_w_out=mix_w_out.astype(BF16),
        ssm_w_in=ssm_w_in.astype(BF16),
        ssm_conv_w=jnp.swapaxes(ssm_conv_w, 1, 2), ssm_conv_b=ssm_conv_b.reshape(n_odd, 1, CONV_DIM),
        ssm_dt_bias=ssm_dt_bias.reshape(n_odd, 1, 2 * SSM_HEADS),
        ssm_A_log=ssm_A_log.reshape(n_odd, 1, 2 * SSM_HEADS),
        ssm_D=jnp.repeat(ssm_D, SSM_HEADDIM, axis=-1).reshape(n_odd, 2, SSM_GROUPS, 1, GROUP_WIDTH),
        ssm_norm_g=ssm_norm_g.reshape(n_odd, SSM_GROUPS, 1, GROUP_WIDTH),
        ssm_w_out=ssm_w_out.astype(BF16),
    )

    y_prompt, ctx = _trunk(x_prompt.reshape(batch * seq, D_MODEL), mods, w, row0=0, n_cond=1,
                           batch=batch, seq=seq)
    y_sample, _ = _trunk(x_sample.reshape(dec_batch * dec_seq, D_MODEL), mods, w, row0=1, n_cond=dec_batch,
                         batch=dec_batch, seq=dec_seq,
                         cache_k=cache_k.reshape(dec_batch, n_even, past, KV_WIDTH),
                         cache_v=cache_v.reshape(dec_batch, n_even, past, KV_WIDTH),
                         state=state_ssm.reshape(dec_batch, n_odd, 2, D_INNER, D_STATE))

    new_k = ctx["k"].reshape(batch, n_even, seq, N_KV_HEADS, HEAD_DIM)
    new_v = ctx["v"].reshape(batch, n_even, seq, N_KV_HEADS, HEAD_DIM)
    new_ssm = ctx["ssm"].reshape(batch, n_odd, 2, SSM_HEADS, SSM_HEADDIM, D_STATE)
    return (y_prompt.reshape(batch, seq, D_MODEL), y_sample.reshape(dec_batch, dec_seq, D_MODEL),
            new_k, new_v, new_ssm)
```

```python
import functools

import jax
import jax.numpy as jnp
from jax import lax
from jax.experimental import pallas as pl
from jax.experimental.pallas import tpu as pltpu

F32 = jnp.float32
BF16 = jnp.bfloat16

D_MODEL = 2048
DEPTH = 2
GRID_W = 64
EPS = 1e-6
N_MOD = 9
N_NORM = 6
D_FF = 5632
POOL_WINDOWS = (2, 4, 8, 16)
POOL_WIDTH = 1024
POOL_GROUP_DIM = 256
HEAD_DIM = 128
N_KV_HEADS = 2
Q_PER_KV = 4
ATTN_WIDTH = 1024
KV_WIDTH = 256
MIX_IN = 2560
ROPE_THETA = 10000.0
D_INNER = 4096
SSM_HEADDIM = 64
SSM_HEADS = 64
SSM_GROUPS = 8
HEADS_PER_GROUP = 8
D_STATE = 128
SSM_CHUNK = 128
CONV_DIM = 6144
GROUP_WIDTH = HEADS_PER_GROUP * SSM_HEADDIM

LANES = 128
VMEM_LIMIT = 60 * 1024 * 1024
COND_ROWS = 16


def _params(sem):
    return pltpu.CompilerParams(dimension_semantics=sem, vmem_limit_bytes=VMEM_LIMIT)


def _rms(x, g):
    ms = jnp.mean(x * x, axis=-1, keepdims=True)
    return x * lax.rsqrt(ms + EPS) * g


def _silu(x):
    return x * jax.nn.sigmoid(x)


ROW_BLOCK = 16


def _modulate_rows(h_ref, g, shift, scale, u_ref):
    gain = g * (1 + scale)
    for r in range(0, h_ref.shape[0], ROW_BLOCK):
        x = h_ref[r:r + ROW_BLOCK]
        ms = jnp.mean(x * x, axis=-1, keepdims=True)
        u_ref[r:r + ROW_BLOCK] = (x * lax.rsqrt(ms + EPS) * gain + shift).astype(BF16)


def _residual_rows(h_ref, y_ref, g, gate, out_ref):
    gain = g * gate
    for r in range(0, h_ref.shape[0], ROW_BLOCK):
        y = y_ref[r:r + ROW_BLOCK]
        ms = jnp.mean(y * y, axis=-1, keepdims=True)
        out_ref[r:r + ROW_BLOCK] = h_ref[r:r + ROW_BLOCK] + y * lax.rsqrt(ms + EPS) * gain


class _Cond:
    def __init__(self, layer, row0, n_cond, seq):
        self.layer, self.row0, self.n_cond, self.seq = layer, row0, n_cond, seq

    def specs(self, tm):
        layer, row0 = self.layer, self.row0
        if self.n_cond == 1:
            row = lambda i: row0
        else:
            assert self.seq % tm == 0
            per_seq = self.seq // tm
            row = lambda i: row0 + i // per_seq
        return [pl.BlockSpec((None, None, N_MOD, D_MODEL), lambda i, *_: (layer, row(i), 0, 0)),
                pl.BlockSpec((None, N_NORM, D_MODEL), lambda *_: (layer, 0, 0))]


def _mods_kernel(cond_ref, w_ref, b_ref, out_ref):
    sc = _silu(cond_ref[...]).astype(BF16)
    w = w_ref[...].astype(BF16)
    out_ref[...] = jnp.dot(sc, w, preferred_element_type=F32) + b_ref[...]


def _ada_mods(cond, ada_w, ada_b):
    tn = 1024
    n_out = N_MOD * D_MODEL
    return pl.pallas_call(
        _mods_kernel,
        grid=(DEPTH, n_out // tn),
        in_specs=[
            pl.BlockSpec((COND_ROWS, D_MODEL), lambda l, n: (0, 0)),
            pl.BlockSpec((None, D_MODEL, tn), lambda l, n: (l, 0, n)),
            pl.BlockSpec((None, 1, tn), lambda l, n: (l, 0, n)),
        ],
        out_specs=pl.BlockSpec((None, COND_ROWS, tn), lambda l, n: (l, 0, n)),
        out_shape=jax.ShapeDtypeStruct((DEPTH, COND_ROWS, n_out), F32),
        compiler_params=_params(("parallel", "parallel")),
        name="ada_mods",
    )(cond, ada_w, ada_b.reshape(DEPTH, 1, n_out))


FFN_SUB = 256


def _ffn_kernel(h_ref, mods_ref, g_ref, wa_ref, wb_ref, wo_ref, out_ref, u_ref, *, mod0, g0):
    j = pl.program_id(1)

    def partial_out():
        u = u_ref[...]
        a = jnp.dot(u, wa_ref[...], preferred_element_type=F32)
        b = jnp.dot(u, wb_ref[...], preferred_element_type=F32)
        hidden = (_silu(a) * b).astype(BF16)
        return jnp.dot(hidden, wo_ref[...], preferred_element_type=F32)

    @pl.when(j == 0)
    def _():
        _modulate_rows(h_ref, g_ref[g0:g0 + 1], mods_ref[mod0:mod0 + 1], mods_ref[mod0 + 1:mod0 + 2], u_ref)
        out_ref[...] = partial_out()

    @pl.when(j > 0)
    def _():
        out_ref[...] += partial_out()

    @pl.when(j == pl.num_programs(1) - 1)
    def _():
        _residual_rows(h_ref, out_ref, g_ref[g0 + 1:g0 + 2], 0.5 * mods_ref[mod0 + 2:mod0 + 3], out_ref)


def _ffn(h, mods, norm_g, w_in, w_out, cond, *, half):
    tokens = h.shape[0]
    tm, tf = 1024, 512
    nf = D_FF // tf
    layer = cond.layer
    return pl.pallas_call(
        functools.partial(_ffn_kernel, mod0=6 * half, g0=4 * half),
        grid=(tokens // tm, nf),
        in_specs=[
            pl.BlockSpec((tm, D_MODEL), lambda i, j: (i, 0)),
            *cond.specs(tm),
            pl.BlockSpec((None, None, D_MODEL, tf), lambda i, j: (layer, half, 0, j)),
            pl.BlockSpec((None, None, D_MODEL, tf), lambda i, j: (layer, half, 0, nf + j)),
            pl.BlockSpec((None, None, tf, D_MODEL), lambda i, j: (layer, half, j, 0)),
        ],
        out_specs=pl.BlockSpec((tm, D_MODEL), lambda i, j: (i, 0)),
        out_shape=jax.ShapeDtypeStruct((tokens, D_MODEL), F32),
        scratch_shapes=[pltpu.VMEM((tm, D_MODEL), BF16)],
        compiler_params=_params(("parallel", "arbitrary")),
        name="ffn",
    )(h, mods, norm_g, w_in, w_in, w_out)


SUBLANES = 8


POOL_ROWS = 64


def _halo_block(p_ref, r, n_rows, cols, seq):
    zeros = jnp.zeros((SUBLANES, cols.stop - cols.start), F32)
    before = zeros if r % seq == 0 else p_ref[r - SUBLANES:r, cols]
    after = zeros if (r + n_rows) % seq == 0 else p_ref[r + n_rows:r + n_rows + SUBLANES, cols]
    return jnp.concatenate([before, p_ref[r:r + n_rows, cols], after], axis=0)


def _pool_delta_rows(p_ref, cols, window, seq, d_ref):
    half = window // 2
    assert half <= SUBLANES and seq % POOL_ROWS == 0
    n = p_ref.shape[0]
    rows = POOL_ROWS + 2 * SUBLANES
    row = lax.broadcasted_iota(jnp.int32, (POOL_ROWS, 1), 0)
    for r in range(0, n, POOL_ROWS):
        blk = _halo_block(p_ref, r, POOL_ROWS, cols, seq)
        fwd = blk
        bwd = blk
        m = 1
        while m < half:
            fwd = fwd + pltpu.roll(fwd, rows - m, axis=0)
            bwd = bwd + pltpu.roll(bwd, m, axis=0)
            m *= 2
        total = (fwd + pltpu.roll(bwd, 1, axis=0))[SUBLANES:SUBLANES + POOL_ROWS]
        pos = row + r % seq
        count = jnp.minimum(pos + half, seq) - jnp.maximum(pos - half, 0)
        d = total / count.astype(F32) - blk[SUBLANES:SUBLANES + POOL_ROWS]
        d_ref[r:r + POOL_ROWS, cols] = d.astype(BF16)


def _rope(x, cos, sin_lo, sin_hi):
    quarter = HEAD_DIM // 4
    return (x * cos + pltpu.roll(x, HEAD_DIM - quarter, axis=1) * sin_lo
            + pltpu.roll(x, quarter, axis=1) * sin_hi)


HEAD_ROWS_ROPE = 256


def _even_in_kernel(h_ref, mods_ref, g_ref, w_ref, pw_ref, ps_ref, qkg_ref, *rest, seq, rope):
    if rope:
        cos_ref, slo_ref, shi_ref, pool_ref, q_ref, k_ref, v_ref, u_ref, p_ref, d_ref = rest
    else:
        pool_ref, q_ref, k_ref, v_ref, u_ref, p_ref, d_ref = rest
    j = pl.program_id(1)
    tm = h_ref.shape[0]
    n_pool = POOL_WIDTH // p_ref.shape[2]
    n_q = ATTN_WIDTH // p_ref.shape[2]

    def project(chunk):
        p_ref[chunk % 2] = jnp.dot(u_ref[...], w_ref[...], preferred_element_type=F32)

    groups = p_ref.shape[2] // POOL_GROUP_DIM

    def pool_deltas(chunk):
        for gi in range(groups):
            cols = slice(gi * POOL_GROUP_DIM, (gi + 1) * POOL_GROUP_DIM)
            _pool_delta_rows(p_ref.at[chunk % 2], cols, POOL_WINDOWS[groups * chunk + gi], seq, d_ref)

    def pool_mix():
        for gi in range(groups):
            cols = slice(gi * POOL_GROUP_DIM, (gi + 1) * POOL_GROUP_DIM)
            y = jnp.dot(d_ref[:, cols], pw_ref[gi], preferred_element_type=F32)
            pool_ref[:, cols] = (y * ps_ref[:, cols]).astype(BF16)

    def heads_epilogue(chunk, n_heads, gain, dst_ref):
        src = p_ref.at[chunk % 2]
        step = HEAD_ROWS_ROPE if rope else ROW_BLOCK
        for r in range(0, tm, step):
            rows = slice(r, r + step)
            for hd in range(n_heads):
                cols = slice(hd * HEAD_DIM, (hd + 1) * HEAD_DIM)
                x = _rms(src[rows, cols], gain)
                if rope:
                    x = _rope(x, cos_ref[rows], slo_ref[rows], shi_ref[rows])
                dst_ref[rows, cols] = x.astype(dst_ref.dtype)

    def kv_epilogue(chunk):
        heads_epilogue(chunk, N_KV_HEADS, qkg_ref[1:2], k_ref)
        src = p_ref.at[chunk % 2]
        for r in range(0, tm, ROW_BLOCK):
            v_ref[r:r + ROW_BLOCK] = src[r:r + ROW_BLOCK, KV_WIDTH:]

    @pl.when(j == 0)
    def _():
        _modulate_rows(h_ref, g_ref[2:3], mods_ref[3:4], mods_ref[4:5], u_ref)
        project(0)

    for chunk in range(1, n_pool + n_q + 2):
        @pl.when(j == chunk)
        def _(chunk=chunk):
            done = chunk - 1
            if done < n_pool:
                pool_deltas(done)
            elif done < n_pool + n_q:
                heads_epilogue(done, Q_PER_KV, qkg_ref[0:1], q_ref)
            else:
                kv_epilogue(done)
            if chunk <= n_pool + n_q:
                project(chunk)
            if done < n_pool:
                pool_mix()


def _rope_tables(seq):
    pos = jnp.arange(seq)
    row = (pos // GRID_W).astype(F32)
    col = (pos % GRID_W).astype(F32)
    quarter = HEAD_DIM // 4
    inv_freq = ROPE_THETA ** (-jnp.arange(quarter, dtype=F32) / quarter)
    ang_r = row[:, None] * inv_freq[None]
    ang_c = col[:, None] * inv_freq[None]
    zero = jnp.zeros_like(ang_r)
    cos = jnp.concatenate([jnp.cos(ang_r), jnp.cos(ang_r), jnp.cos(ang_c), jnp.cos(ang_c)], axis=-1)
    sin_lo = jnp.concatenate([-jnp.sin(ang_r), zero, -jnp.sin(ang_c), zero], axis=-1)
    sin_hi = jnp.concatenate([zero, jnp.sin(ang_r), zero, jnp.sin(ang_c)], axis=-1)
    return cos, sin_lo, sin_hi


def _even_in(h, mods, norm_g, w_in, pool_w, pool_scale, qk_g, cond, *, e, rope):
    tokens = h.shape[0]
    seq = cond.seq
    tm, tn = 1024, 512
    assert tm % seq == 0 and tokens % tm == 0
    n_pool = POOL_WIDTH // tn
    n_q = ATTN_WIDTH // tn
    n_chunks = MIX_IN // tn
    pool_idx = lambda j: jnp.clip(j - 1, 0, n_pool - 1)
    q_idx = lambda j: jnp.clip(j - 1 - n_pool, 0, n_q - 1)
    in_specs = [
        pl.BlockSpec((tm, D_MODEL), lambda i, j: (i, 0)),
        *cond.specs(tm),
        pl.BlockSpec((None, D_MODEL, tn), lambda i, j: (e, 0, jnp.minimum(j, n_chunks - 1))),
        pl.BlockSpec((None, 2, POOL_GROUP_DIM, POOL_GROUP_DIM), lambda i, j: (e, pool_idx(j), 0, 0)),
        pl.BlockSpec((None, 1, tn), lambda i, j: (e, 0, pool_idx(j))),
        pl.BlockSpec((None, 2, HEAD_DIM), lambda i, j: (e, 0, 0)),
    ]
    args = [h, mods, norm_g, w_in, pool_w, pool_scale, qk_g]
    if rope:
        assert tm == seq
        in_specs += [pl.BlockSpec((tm, HEAD_DIM), lambda i, j: (0, 0))] * 3
        args += list(_rope_tables(seq))
    return pl.pallas_call(
        functools.partial(_even_in_kernel, seq=seq, rope=rope),
        grid=(tokens // tm, n_chunks + 1),
        in_specs=in_specs,
        out_specs=[
            pl.BlockSpec((tm, tn), lambda i, j: (i, pool_idx(j))),
            pl.BlockSpec((tm, tn), lambda i, j: (i, q_idx(j))),
            pl.BlockSpec((tm, KV_WIDTH), lambda i, j: (i, 0)),
            pl.BlockSpec((tm, KV_WIDTH), lambda i, j: (i, 0)),
        ],
        out_shape=[
            jax.ShapeDtypeStruct((tokens, POOL_WIDTH), BF16),
            jax.ShapeDtypeStruct((tokens, ATTN_WIDTH), BF16),
            jax.ShapeDtypeStruct((tokens, KV_WIDTH), F32),
            jax.ShapeDtypeStruct((tokens, KV_WIDTH), F32),
        ],
        scratch_shapes=[pltpu.VMEM((tm, D_MODEL), BF16), pltpu.VMEM((2, tm, tn), F32),
                        pltpu.VMEM((tm, tn), BF16)],
        compiler_params=_params(("parallel", "arbitrary")),
        name="even_in",
    )(*args)


def _attn_kernel(q_ref, k_ref, v_ref, *rest, cached):
    if cached:
        ck_ref, cv_ref, o_ref = rest
    else:
        (o_ref,) = rest
    scale = HEAD_DIM ** -0.5
    nt = (((1,), (1,)), ((), ()))
    kb = k_ref[...].astype(BF16)
    vb = v_ref[...].astype(BF16)
    if cached:
        ckb = ck_ref[...].astype(BF16)
        cvb = cv_ref[...].astype(BF16)
    for hd in range(Q_PER_KV):
        sl = slice(hd * HEAD_DIM, (hd + 1) * HEAD_DIM)
        q = q_ref[:, sl]
        s = lax.dot_general(q, kb, nt, preferred_element_type=F32) * scale
        m = jnp.max(s, axis=-1, keepdims=True)
        if cached:
            sc = lax.dot_general(q, ckb, nt, preferred_element_type=F32) * scale
            m = jnp.maximum(m, jnp.max(sc, axis=-1, keepdims=True))
        e = jnp.exp(s - m)
        denom = jnp.sum(e, axis=-1, keepdims=True)
        o = jnp.dot(e.astype(BF16), vb, preferred_element_type=F32)
        if cached:
            ec = jnp.exp(sc - m)
            denom = denom + jnp.sum(ec, axis=-1, keepdims=True)
            o = o + jnp.dot(ec.astype(BF16), cvb, preferred_element_type=F32)
        o_ref[:, sl] = (o / denom).astype(BF16)


def _attention(q, k, v, cache_k, cache_v, *, e, batch, seq):
    tokens = q.shape[0]
    tq = 256
    nq = seq // tq
    gw = Q_PER_KV * HEAD_DIM
    cached = cache_k is not None
    in_specs = [
        pl.BlockSpec((tq, gw), lambda b, g, i: (b * nq + i, g)),
        pl.BlockSpec((seq, HEAD_DIM), lambda b, g, i: (b, g)),
        pl.BlockSpec((seq, HEAD_DIM), lambda b, g, i: (b, g)),
    ]
    args = [q, k, v]
    if cached:
        past = cache_k.shape[2]
        in_specs += [pl.BlockSpec((None, None, past, HEAD_DIM), lambda b, g, i: (b, e, 0, g))] * 2
        args += [cache_k, cache_v]
    return pl.pallas_call(
        functools.partial(_attn_kernel, cached=cached),
        grid=(batch, N_KV_HEADS, nq),
        in_specs=in_specs,
        out_specs=pl.BlockSpec((tq, gw), lambda b, g, i: (b * nq + i, g)),
        out_shape=jax.ShapeDtypeStruct((tokens, ATTN_WIDTH), BF16),
        compiler_params=_params(("parallel", "parallel", "parallel")),
        name="attention",
    )(*args)


def _even_out_kernel(pool_ref, attn_ref, w_ref, h_ref, mods_ref, g_ref, out_ref):
    y = jnp.dot(pool_ref[...], w_ref[:POOL_WIDTH], preferred_element_type=F32)
    y = y + jnp.dot(attn_ref[...], w_ref[POOL_WIDTH:], preferred_element_type=F32)
    out_ref[...] = y
    _residual_rows(h_ref, out_ref, g_ref[3:4], mods_ref[5:6], out_ref)


def _even_out(pool_y, attn, w_out, h, mods, norm_g, cond, *, e):
    tokens = h.shape[0]
    tm = 512
    return pl.pallas_call(
        _even_out_kernel,
        grid=(tokens // tm,),
        in_specs=[
            pl.BlockSpec((tm, POOL_WIDTH), lambda i: (i, 0)),
            pl.BlockSpec((tm, ATTN_WIDTH), lambda i: (i, 0)),
            pl.BlockSpec((None, POOL_WIDTH + ATTN_WIDTH, D_MODEL), lambda i: (e, 0, 0)),
            pl.BlockSpec((tm, D_MODEL), lambda i: (i, 0)),
            *cond.specs(tm),
        ],
        out_specs=pl.BlockSpec((tm, D_MODEL), lambda i: (i, 0)),
        out_shape=jax.ShapeDtypeStruct((tokens, D_MODEL), F32),
        compiler_params=_params(("parallel",)),
        name="even_out",
    )(pool_y, attn, w_out, h, mods, norm_g)


def _odd_out_kernel(x_ref, w_ref, h_ref, mods_ref, g_ref, out_ref):
    k = pl.program_id(1)
    y = jnp.dot(x_ref[0], w_ref[:GROUP_WIDTH], preferred_element_type=F32)
    for g in range(1, x_ref.shape[0]):
        y = y + jnp.dot(x_ref[g], w_ref[g * GROUP_WIDTH:(g + 1) * GROUP_WIDTH], preferred_element_type=F32)

    @pl.when(k == 0)
    def _():
        out_ref[...] = y

    @pl.when(k == pl.num_programs(1) - 1)
    def _():
        out_ref[...] += y
        _residual_rows(h_ref, out_ref, g_ref[3:4], mods_ref[5:6], out_ref)


def _odd_out(yn, w_out, h, mods, norm_g, cond, *, o):
    tokens = h.shape[0]
    tm, nk = 512, 2
    gk = SSM_GROUPS // nk
    return pl.pallas_call(
        _odd_out_kernel,
        grid=(tokens // tm, nk),
        in_specs=[
            pl.BlockSpec((gk, tm, GROUP_WIDTH), lambda i, k: (k, i, 0)),
            pl.BlockSpec((None, gk * GROUP_WIDTH, D_MODEL), lambda i, k: (o, k, 0)),
            pl.BlockSpec((tm, D_MODEL), lambda i, k: (i, 0)),
            *cond.specs(tm),
        ],
        out_specs=pl.BlockSpec((tm, D_MODEL), lambda i, k: (i, 0)),
        out_shape=jax.ShapeDtypeStruct((tokens, D_MODEL), F32),
        compiler_params=_params(("parallel", "arbitrary")),
        name="odd_out",
    )(yn, w_out, h, mods, norm_g)


N_Z = D_INNER // GROUP_WIDTH
N_XBC = CONV_DIM // GROUP_WIDTH
X_CHUNKS = D_INNER // GROUP_WIDTH
BC_CHUNKS = SSM_GROUPS * D_STATE // GROUP_WIDTH
GROUPS_PER_CHUNK = GROUP_WIDTH // D_STATE
ODD_IN_CHUNKS = 2


def _conv_silu_rows(p_ref, cols, cw_ref, cb_ref, out_ref, seq):
    n = p_ref.shape[0]
    row = lax.broadcasted_iota(jnp.int32, (ROW_BLOCK, 1), 0)
    w_prev, w_cur, w_next, bias = cw_ref[0:1, cols], cw_ref[1:2, cols], cw_ref[2:3, cols], cb_ref[:, cols]
    for r in range(0, n, ROW_BLOCK):
        lo, hi = max(r - SUBLANES, 0), min(r + ROW_BLOCK + SUBLANES, n)
        blk = p_ref[lo:hi, cols]
        cur = blk[r - lo:r - lo + ROW_BLOCK]
        prev = pltpu.roll(blk, 1, axis=0)[r - lo:r - lo + ROW_BLOCK]
        nxt = pltpu.roll(blk, hi - lo - 1, axis=0)[r - lo:r - lo + ROW_BLOCK]
        if r % seq == 0:
            prev = jnp.where(row == 0, 0.0, prev)
        if (r + ROW_BLOCK) % seq == 0:
            nxt = jnp.where(row == ROW_BLOCK - 1, 0.0, nxt)
        out_ref[r:r + ROW_BLOCK] = _silu(prev * w_prev + cur * w_cur + nxt * w_next + bias)


def _odd_in_kernel(h_ref, mods_ref, g_ref, w_ref, wdt_ref, cw_ref, cb_ref, dtb_ref,
                   z_ref, xbc_ref, dt_ref, u_ref, p_ref, *, seq):
    j = pl.program_id(1)

    def project():
        return jnp.dot(u_ref[...], w_ref[...], preferred_element_type=F32)

    def chunk_cols(c):
        return slice(c * GROUP_WIDTH, (c + 1) * GROUP_WIDTH)

    @pl.when(j == 0)
    def _():
        _modulate_rows(h_ref, g_ref[2:3], mods_ref[3:4], mods_ref[4:5], u_ref)
        raw = jnp.dot(u_ref[...], wdt_ref[...], preferred_element_type=F32) + dtb_ref[...]
        dt_ref[...] = jnp.maximum(raw, 0.0) + jnp.log1p(jnp.exp(-jnp.abs(raw)))

    @pl.when(j < N_Z // ODD_IN_CHUNKS)
    def _():
        p = project()
        for c in range(ODD_IN_CHUNKS):
            z_ref[c] = p[:, chunk_cols(c)]

    @pl.when(j >= N_Z // ODD_IN_CHUNKS)
    def _():
        p_ref[...] = project()
        for c in range(ODD_IN_CHUNKS):
            _conv_silu_rows(p_ref, chunk_cols(c), cw_ref, cb_ref, xbc_ref.at[c], seq)


def _odd_in(h, mods, norm_g, w_in, conv_w, conv_b, dt_bias, cond, *, o):
    tokens = h.shape[0]
    seq = cond.seq
    tm, tn = 1024, ODD_IN_CHUNKS * GROUP_WIDTH
    assert tm % seq == 0 and tokens % tm == 0
    assert N_Z % ODD_IN_CHUNKS == 0 and N_XBC % ODD_IN_CHUNKS == 0
    nz = N_Z // ODD_IN_CHUNKS
    dt_block = (D_INNER + CONV_DIM) // LANES
    conv_idx = lambda j: jnp.maximum(j - nz, 0)
    return pl.pallas_call(
        functools.partial(_odd_in_kernel, seq=seq),
        grid=(tokens // tm, (N_Z + N_XBC) // ODD_IN_CHUNKS),
        in_specs=[
            pl.BlockSpec((tm, D_MODEL), lambda i, j: (i, 0)),
            *cond.specs(tm),
            pl.BlockSpec((None, D_MODEL, tn), lambda i, j: (o, 0, j)),
            pl.BlockSpec((None, D_MODEL, 2 * SSM_HEADS), lambda i, j: (o, 0, dt_block)),
            pl.BlockSpec((None, 3, tn), lambda i, j: (o, 0, conv_idx(j))),
            pl.BlockSpec((None, 1, tn), lambda i, j: (o, 0, conv_idx(j))),
            pl.BlockSpec((None, 1, 2 * SSM_HEADS), lambda i, j: (o, 0, 0)),
        ],
        out_specs=[
            pl.BlockSpec((ODD_IN_CHUNKS, tm, GROUP_WIDTH), lambda i, j: (jnp.minimum(j, nz - 1), i, 0)),
            pl.BlockSpec((ODD_IN_CHUNKS, tm, GROUP_WIDTH), lambda i, j: (conv_idx(j), i, 0)),
            pl.BlockSpec((tm, 2 * SSM_HEADS), lambda i, j: (i, 0)),
        ],
        out_shape=[
            jax.ShapeDtypeStruct((N_Z, tokens, GROUP_WIDTH), F32),
            jax.ShapeDtypeStruct((N_XBC, tokens, GROUP_WIDTH), F32),
            jax.ShapeDtypeStruct((tokens, 2 * SSM_HEADS), F32),
        ],
        scratch_shapes=[pltpu.VMEM((tm, D_MODEL), BF16), pltpu.VMEM((tm, tn), F32)],
        compiler_params=_params(("parallel", "arbitrary")),
        name="odd_in",
    )(h, mods, norm_g, w_in, w_in, conv_w, conv_b, dt_bias)


def _split3(x):
    x1 = x.astype(BF16)
    r1 = x - x1.astype(F32)
    x2 = r1.astype(BF16)
    x3 = (r1 - x2.astype(F32)).astype(BF16)
    return x1, x2, x3


def _ssd_kernel(*refs, direction, has_h0, emit_state, combine):
    refs = list(refs)
    x_ref, b_ref, c_ref, dt_ref, alog_ref = refs[:5]
    del refs[:5]
    h0_ref = refs.pop(0) if has_h0 else None
    if combine:
        yf_ref, z_ref, dskip_ref, ng_ref = refs[:4]
        del refs[:4]
    other_state_ref = refs.pop(0) if (emit_state and combine) else None
    y_ref = refs.pop(0)
    state_out_ref = refs.pop(0) if emit_state else None
    ht_ref = refs.pop(0)
    yb_ref = refs.pop(0) if combine else None

    c = pl.program_id(1)
    q = SSM_CHUNK
    pairs = GROUP_WIDTH // LANES

    @pl.when(c == 0)
    def _():
        if has_h0:
            for g in range(SSM_GROUPS):
                for pr in range(pairs):
                    r0 = g * GROUP_WIDTH + pr * LANES
                    ht_ref[g, :, pr * LANES:(pr + 1) * LANES] = h0_ref[r0:r0 + LANES, :].T
        else:
            ht_ref[...] = jnp.zeros_like(ht_ref)

    ii = lax.broadcasted_iota(jnp.int32, (q, q), 0)
    jj = lax.broadcasted_iota(jnp.int32, (q, q), 1)
    causal = (jj <= ii) if direction == 0 else (jj >= ii)
    tri = jnp.where(causal, 1.0, 0.0).astype(BF16)
    low_lane = lax.broadcasted_iota(jnp.int32, (1, LANES), 1) < SSM_HEADDIM

    dt = dt_ref[...]
    dta = dt * (-jnp.exp(alog_ref[...]))
    a = sum(jnp.dot(tri, part, preferred_element_type=F32) for part in _split3(dta))
    a_end = a[q - 1:q] if direction == 0 else a[0:1]
    a_t = a.T
    dt_t = dt.T
    s_t = (dt * jnp.exp(a_end - a)).T
    end_scale = jnp.exp(a_end)

    for g in range(SSM_GROUPS):
        bc_lanes = slice((g % GROUPS_PER_CHUNK) * D_STATE, (g % GROUPS_PER_CHUNK + 1) * D_STATE)
        bg_t = b_ref[g // GROUPS_PER_CHUNK, :, bc_lanes].T
        cg32 = c_ref[g // GROUPS_PER_CHUNK, :, bc_lanes]
        cb = jnp.dot(cg32.astype(BF16), bg_t.astype(BF16), preferred_element_type=F32)
        for pr in range(pairs):
            lanes = slice(pr * LANES, (pr + 1) * LANES)
            xp = x_ref[g, :, lanes].astype(BF16)
            hp = ht_ref[g, :, lanes]
            hpb = hp.astype(BF16)
            ys, ds, scales = [], [], []
            for e in range(2):
                hl = direction * SSM_HEADS + g * HEADS_PER_GROUP + 2 * pr + e
                col = jnp.broadcast_to(a[:, hl:hl + 1], (q, q))
                decay = jnp.exp(jnp.where(causal, col - a_t[hl:hl + 1, :], -jnp.inf))
                w = (cb * decay * dt_t[hl:hl + 1, :]).astype(BF16)
                cdec = (cg32 * jnp.exp(col)).astype(BF16)
                ys.append(jnp.dot(w, xp, preferred_element_type=F32)
                          + jnp.dot(cdec, hpb, preferred_element_type=F32))
                bs = (bg_t * s_t[hl:hl + 1, :]).astype(BF16)
                ds.append(jnp.dot(bs, xp, preferred_element_type=F32))
                scales.append(jnp.broadcast_to(end_scale[:, hl:hl + 1], (1, LANES)))
            y_pair = jnp.where(low_lane, ys[0], ys[1])
            if combine:
                yb_ref[g, :, lanes] = y_pair
            else:
                y_ref[g, :, lanes] = y_pair
            ht_ref[g, :, lanes] = (hp * jnp.where(low_lane, scales[0], scales[1])
                                   + jnp.where(low_lane, ds[0], ds[1]))

    if combine:
        ssq = jnp.zeros((q, 1), F32)
        for g in range(SSM_GROUPS):
            xg = x_ref[g]
            y = (yf_ref[g] + dskip_ref[0, g] * xg) + (yb_ref[g] + dskip_ref[1, g] * xg)
            y = y * _silu(z_ref[g])
            yb_ref[g] = y
            ssq = ssq + jnp.sum(y * y, axis=-1, keepdims=True)
        inv = lax.rsqrt(ssq / D_INNER + EPS)
        for g in range(SSM_GROUPS):
            y_ref[g] = (yb_ref[g] * inv * ng_ref[g]).astype(BF16)

    if emit_state:
        @pl.when(c == pl.num_programs(1) - 1)
        def _():
            own = state_out_ref.at[direction] if combine else state_out_ref
            for g in range(SSM_GROUPS):
                for pr in range(pairs):
                    r0 = g * GROUP_WIDTH + pr * LANES
                    own[r0:r0 + LANES, :] = ht_ref[g, :, pr * LANES:(pr + 1) * LANES].T
            if combine:
                state_out_ref[1 - direction] = other_state_ref[...]


def _ssd(xbc, dt, a_log, h0, *, o, direction, batch, seq, emit_state, combine=None, other_state=None):
    tokens = xbc.shape[1]
    q = SSM_CHUNK
    nc = seq // q
    if direction == 0:
        chunk = lambda b, c: b * nc + c
    else:
        chunk = lambda b, c: b * nc + (nc - 1 - c)
    rows = lambda n, first: pl.BlockSpec((n, q, GROUP_WIDTH), lambda b, c: (first // n, chunk(b, c), 0))
    in_specs = [rows(X_CHUNKS, 0), rows(BC_CHUNKS, X_CHUNKS), rows(BC_CHUNKS, X_CHUNKS + BC_CHUNKS),
                pl.BlockSpec((q, 2 * SSM_HEADS), lambda b, c: (chunk(b, c), 0)),
                pl.BlockSpec((None, 1, 2 * SSM_HEADS), lambda b, c: (o, 0, 0))]
    args = [xbc, xbc, xbc, dt, a_log]
    if h0 is not None:
        in_specs.append(pl.BlockSpec((None, None, None, D_INNER, D_STATE), lambda b, c: (b, o, direction, 0, 0)))
        args.append(h0)
    if combine is not None:
        y_f, z, d_skip, norm_g = combine
        in_specs += [rows(X_CHUNKS, 0), rows(N_Z, 0),
                     pl.BlockSpec((None, 2, SSM_GROUPS, 1, GROUP_WIDTH), lambda b, c: (o, 0, 0, 0, 0)),
                     pl.BlockSpec((None, SSM_GROUPS, 1, GROUP_WIDTH), lambda b, c: (o, 0, 0, 0))]
        args += [y_f, z, d_skip, norm_g]
    out_specs = [rows(X_CHUNKS, 0)]
    out_shape = [jax.ShapeDtypeStruct((X_CHUNKS, tokens, GROUP_WIDTH), BF16 if combine is not None else F32)]
    if emit_state and combine is not None:
        in_specs.append(pl.BlockSpec((None, D_INNER, D_STATE), lambda b, c: (b, 0, 0)))
        args.append(other_state)
        out_specs.append(pl.BlockSpec((None, 2, D_INNER, D_STATE), lambda b, c: (b, 0, 0, 0)))
        out_shape.append(jax.ShapeDtypeStruct((batch, 2, D_INNER, D_STATE), F32))
    elif emit_state:
        out_specs.append(pl.BlockSpec((None, D_INNER, D_STATE), lambda b, c: (b, 0, 0)))
        out_shape.append(jax.ShapeDtypeStruct((batch, D_INNER, D_STATE), F32))
    scratch = [pltpu.VMEM((SSM_GROUPS, D_STATE, GROUP_WIDTH), F32)]
    if combine is not None:
        scratch.append(pltpu.VMEM((SSM_GROUPS, q, GROUP_WIDTH), F32))
    return pl.pallas_call(
        functools.partial(_ssd_kernel, direction=direction, has_h0=h0 is not None,
                          emit_state=emit_state, combine=combine is not None),
        grid=(batch, nc),
        in_specs=in_specs,
        out_specs=out_specs,
        out_shape=out_shape,
        scratch_shapes=scratch,
        compiler_params=_params(("parallel", "arbitrary")),
        name="ssd_fwd" if direction == 0 else "ssd_bwd",
    )(*args)


def _trunk(h, mods, w, *, row0, n_cond, batch, seq, cache_k=None, cache_v=None, state=None):
    context = cache_k is None
    outs = {}
    for l in range(DEPTH):
        cond = _Cond(l, row0, n_cond, seq)
        norm_g = w["norm_g"]
        h = _ffn(h, mods, norm_g, w["ffn_w_in"], w["ffn_w_out"], cond, half=0)
        if l % 2 == 0:
            e = l // 2
            pool_y, q, k, v = _even_in(h, mods, norm_g, w["mix_w_in"], w["pool_w"], w["pool_scale"],
                                       w["qk_norm_g"], cond, e=e, rope=not context)
            attn = _attention(q, k, v, cache_k, cache_v, e=e, batch=batch, seq=seq)
            h = _even_out(pool_y, attn, w["mix_w_out"], h, mods, norm_g, cond, e=e)
            outs["k"], outs["v"] = k, v
        else:
            o = l // 2
            z, xbc, dt = _odd_in(h, mods, norm_g, w["ssm_w_in"], w["ssm_conv_w"], w["ssm_conv_b"],
                                 w["ssm_dt_bias"], cond, o=o)
            scan = functools.partial(_ssd, xbc, dt, w["ssm_A_log"], state, o=o, batch=batch, seq=seq)
            fwd = scan(direction=0, emit_state=context)
            bwd = scan(direction=1, emit_state=context, other_state=fwd[1] if context else None,
                       combine=(fwd[0], z, w["ssm_D"], w["ssm_norm_g"]))
            if context:
                outs["ssm"] = bwd[1]
            h = _odd_out(bwd[0], w["ssm_w_out"], h, mods, norm_g, cond, o=o)
        h = _ffn(h, mods, norm_g, w["ffn_w_in"], w["ffn_w_out"], cond, half=1)
    return h, outs


def kernel(x_prompt, x_sample, cache_k, cache_v, state_ssm, c, c_ctx, ada_w, ada_b, norm_g, ffn_w_in,
           ffn_w_out, mix_w_in, pool_w, pool_scale, qk_norm_g, mix_w_out, ssm_w_in, ssm_conv_w,
           ssm_conv_b, ssm_dt_bias, ssm_A_log, ssm_D, ssm_norm_g, ssm_w_out):
    batch, seq, _ = x_prompt.shape
    dec_batch, dec_seq, _ = x_sample.shape
    n_even, n_odd = mix_w_in.shape[0], ssm_w_in.shape[0]
    past = cache_k.shape[2]

    cond = jnp.zeros((COND_ROWS, D_MODEL), F32).at[0].set(c_ctx).at[1:1 + dec_batch].set(c)
    mods = _ada_mods(cond, ada_w, ada_b).reshape(DEPTH, COND_ROWS, N_MOD, D_MODEL)

    w = dict(
        norm_g=norm_g,
        ffn_w_in=ffn_w_in.astype(BF16), ffn_w_out=ffn_w_out.astype(BF16),
        mix_w_in=mix_w_in.astype(BF16), pool_w=pool_w.astype(BF16),
        pool_scale=pool_scale.reshape(n_even, 1, POOL_WIDTH), qk_norm_g=qk_norm_g,
        mix_w_out=mix_w_out.astype(BF16),
        ssm_w_in=ssm_w_in.astype(BF16),
        ssm_conv_w=jnp.swapaxes(ssm_conv_w, 1, 2), ssm_conv_b=ssm_conv_b.reshape(n_odd, 1, CONV_DIM),
        ssm_dt_bias=ssm_dt_bias.reshape(n_odd, 1, 2 * SSM_HEADS),
        ssm_A_log=ssm_A_log.reshape(n_odd, 1, 2 * SSM_HEADS),
        ssm_D=jnp.repeat(ssm_D, SSM_HEADDIM, axis=-1).reshape(n_odd, 2, SSM_GROUPS, 1, GROUP_WIDTH),
        ssm_norm_g=ssm_norm_g.reshape(n_odd, SSM_GROUPS, 1, GROUP_WIDTH),
        ssm_w_out=ssm_w_out.astype(BF16),
    )

    y_prompt, ctx = _trunk(x_prompt.reshape(batch * seq, D_MODEL), mods, w, row0=0, n_cond=1,
                           batch=batch, seq=seq)
    y_sample, _ = _trunk(x_sample.reshape(dec_batch * dec_seq, D_MODEL), mods, w, row0=1, n_cond=dec_batch,
                         batch=dec_batch, seq=dec_seq,
                         cache_k=cache_k.reshape(dec_batch, n_even, past, KV_WIDTH),
                         cache_v=cache_v.reshape(dec_batch, n_even, past, KV_WIDTH),
                         state=state_ssm.reshape(dec_batch, n_odd, 2, D_INNER, D_STATE))

    new_k = ctx["k"].reshape(batch, n_even, seq, N_KV_HEADS, HEAD_DIM)
    new_v = ctx["v"].reshape(batch, n_even, seq, N_KV_HEADS, HEAD_DIM)
    new_ssm = ctx["ssm"].reshape(batch, n_odd, 2, SSM_HEADS, SSM_HEADDIM, D_STATE)
    return (y_prompt.reshape(batch, seq, D_MODEL), y_sample.reshape(dec_batch, dec_seq, D_MODEL),
            new_k, new_v, new_ssm)
```
